```python
import math
import jax
import jax.numpy as jnp
from jax import lax
import numpy as np

D_MODEL = 2048
BATCH = 2
SEQ = 8192
DEPTH = 2

N_META = 16
NORM_EPS = 1e-6
L2_EPS = 1e-6

GDN_HEADS = 8
GDN_DK = 128
GDN_DV = 128
GDN_CONV = 4
GDN_CHUNK = 64

MLA_HEADS = 8
MLA_Q_LORA = 768
MLA_KV_LORA = 512
MLA_D_NOPE = 128
MLA_D_ROPE = 64
MLA_DV = 128
ROPE_THETA = 10000.0
ATTN_BLOCK = 128

RWKV_HEADS = 16
RWKV_N = 64
RWKV_W = RWKV_HEADS * RWKV_N
RWKV_DECAY_LORA = 64
RWKV_A_LORA = 64
RWKV_G_LORA = 160
RWKV_LN_EPS = 64e-5
RWKV_IN = 3 * RWKV_W + RWKV_DECAY_LORA + RWKV_A_LORA + RWKV_G_LORA

N_BRANCH = 3
BRANCH_W = 1024

N_GROUPS = 4
EXPERTS_PER_GROUP = 8
N_EXPERTS = N_GROUPS * EXPERTS_PER_GROUP
TOP_K = 2
D_EXPERT = 512
MOE_BLOCK = 128

GDN_QK = GDN_HEADS * GDN_DK
GDN_VW = GDN_HEADS * GDN_DV
IN_SPLITS = (GDN_QK, GDN_QK, GDN_VW, GDN_VW, GDN_HEADS, GDN_HEADS,
             MLA_Q_LORA, MLA_KV_LORA, MLA_D_ROPE, RWKV_IN, N_BRANCH * D_MODEL)
D_IN = sum(IN_SPLITS)

kernel_name = 'hybrid_gdn_mla_rwkv7_hmoe'

F32 = jnp.float32


def _split(x, sizes):
    return jnp.split(x, np.cumsum(sizes)[:-1].tolist(), axis=-1)


def rmsnorm(x, w, eps=NORM_EPS):
    x32 = x.astype(F32)
    y = x32 * lax.rsqrt(jnp.mean(jnp.square(x32), axis=-1, keepdims=True) + eps)
    return (y * w.astype(F32)).astype(x.dtype)


def l2norm(x, eps=L2_EPS):
    return x * lax.rsqrt(jnp.sum(jnp.square(x), axis=-1, keepdims=True) + eps)


def causal_depthwise_conv(x, w):
    K, C = w.shape
    return lax.conv_general_dilated(
        x, w[:, None, :].astype(x.dtype), window_strides=(1,), padding=[(K - 1, 0)],
        dimension_numbers=('NWC', 'WIO', 'NWC'), feature_group_count=C)


def rotary(x, cos, sin):
    x1, x2 = jnp.split(x, 2, axis=-1)
    return jnp.concatenate([x1 * cos - x2 * sin, x2 * cos + x1 * sin], axis=-1)


def gated_deltanet(q, k, v, z, b, a, conv_w, a_log, dt_bias, norm_w):
    B, T, _ = q.shape
    H, DK, DV, C = GDN_HEADS, GDN_DK, GDN_DV, GDN_CHUNK
    qkv = jax.nn.silu(causal_depthwise_conv(jnp.concatenate([q, k, v], axis=-1), conv_w))
    q, k, v = _split(qkv.astype(F32), (GDN_QK, GDN_QK, GDN_VW))
    q = l2norm(q.reshape(B, T, H, DK)) * DK ** -0.5
    k = l2norm(k.reshape(B, T, H, DK))
    v = v.reshape(B, T, H, DV)
    beta = jax.nn.sigmoid(b.astype(F32))
    g = -jnp.exp(a_log.astype(F32)) * jax.nn.softplus(a.astype(F32) + dt_bias.astype(F32))
    front = (-N_META) % C
    n_chunks = -(-(T + front) // C)
    back = n_chunks * C - T - front

    def to_chunks(t):
        t = jnp.pad(t, [(0, 0), (front, back)] + [(0, 0)] * (t.ndim - 2))
        return jnp.moveaxis(t.reshape((B, n_chunks, C) + t.shape[2:]), 3, 1)

    q, k, v, beta, g = [to_chunks(t) for t in (q, k, v, beta, g)]
    gc = jnp.cumsum(g, axis=-1)
    causal = jnp.tril(jnp.ones((C, C), bool))
    strict = jnp.tril(jnp.ones((C, C), bool), -1)
    decay = jnp.where(causal, jnp.exp(jnp.where(causal, gc[..., :, None] - gc[..., None, :], 0.0)), 0.0)
    kb = k * beta[..., None]
    A = jnp.where(strict, jnp.einsum('bhncd,bhnsd->bhncs', kb, k) * decay, 0.0)
    eye = jnp.eye(C, dtype=F32)
    t_inv = lax.linalg.triangular_solve(A + eye, jnp.broadcast_to(eye, A.shape),
                                        left_side=True, lower=True, unit_diagonal=True)
    u = t_inv @ (v * beta[..., None])
    w = t_inv @ (kb * jnp.exp(gc)[..., None])
    attn = jnp.einsum('bhncd,bhnsd->bhncs', q, k) * decay
    q_dec = q * jnp.exp(gc)[..., None]
    k_dec = k * jnp.exp(gc[..., -1:] - gc)[..., None]
    g_last = jnp.exp(gc[..., -1])
    xs = tuple(jnp.moveaxis(t, 2, 0) for t in (u, w, attn, q_dec, k_dec, g_last))

    def chunk_step(S, inp):
        u_n, w_n, attn_n, q_n, k_n, gl_n = inp
        v_new = u_n - jnp.einsum('bhcd,bhde->bhce', w_n, S)
        o = jnp.einsum('bhcd,bhde->bhce', q_n, S) + jnp.einsum('bhcs,bhse->bhce', attn_n, v_new)
        S = S * gl_n[..., None, None] + jnp.einsum('bhcd,bhce->bhde', k_n, v_new)
        return S, o

    _, o = lax.scan(chunk_step, jnp.zeros((B, H, DK, DV), F32), xs)
    o = jnp.moveaxis(o, 0, 2).reshape(B, H, n_chunks * C, DV)
    o = jnp.transpose(o, (0, 2, 1, 3))[:, front:front + T]
    o = rmsnorm(o, norm_w) * jax.nn.silu(z.reshape(B, T, H, DV).astype(F32))
    return o.reshape(B, T, GDN_VW).astype(z.dtype)


def causal_block_attention(q_nope, q_rope, k_nope, k_rope, v, scale):
    B, T, H, _ = q_nope.shape
    n_blk = -(-T // ATTN_BLOCK)
    pad = n_blk * ATTN_BLOCK - T

    def blocks(t):
        t = jnp.pad(t, [(0, 0), (0, pad), (0, 0), (0, 0)])
        return jnp.moveaxis(t.reshape((B, n_blk, ATTN_BLOCK) + t.shape[2:]), 1, 0)

    key_pos = jnp.arange(T)

    def one_block(args):
        qn, qr, start = args
        s = (jnp.einsum('bqhd,bkhd->bhqk', qn, k_nope)
             + jnp.einsum('bqhd,bkd->bhqk', qr, k_rope)).astype(F32) * scale
        q_pos = start + jnp.arange(ATTN_BLOCK)
        s = jnp.where(key_pos[None, :] <= q_pos[:, None], s, -jnp.inf)
        p = jax.nn.softmax(s, axis=-1)
        return jnp.einsum('bhqk,bkhd->bqhd', p.astype(v.dtype), v)

    starts = jnp.arange(n_blk) * ATTN_BLOCK
    out = lax.map(one_block, (blocks(q_nope), blocks(q_rope), starts))
    return jnp.moveaxis(out, 0, 1).reshape(B, n_blk * ATTN_BLOCK, H, v.shape[-1])[:, :T]


def mla(cq, ckv, k_pe, q_norm, w_q_up, kv_norm, w_kv_up, cos, sin):
    B, T, _ = cq.shape
    H = MLA_HEADS
    q = (rmsnorm(cq, q_norm) @ w_q_up).reshape(B, T, H, MLA_D_NOPE + MLA_D_ROPE).astype(F32)
    q_nope, q_pe = q[..., :MLA_D_NOPE], rotary(q[..., MLA_D_NOPE:], cos, sin)
    kv = (rmsnorm(ckv, kv_norm) @ w_kv_up).reshape(B, T, H, MLA_D_NOPE + MLA_DV).astype(F32)
    k_nope, v = kv[..., :MLA_D_NOPE], kv[..., MLA_D_NOPE:]
    k_pe = rotary(k_pe.astype(F32)[:, :, None, :], cos, sin)[:, :, 0]
    o = causal_block_attention(q_nope, q_pe, k_nope, k_pe, v, (MLA_D_NOPE + MLA_D_ROPE) ** -0.5)
    return o.reshape(B, T, H * MLA_DV).astype(cq.dtype)


def rwkv7_time_mix(xin, mu, w0, w2, a0, a2, g2, k_k, k_a, r_k, ln_w, ln_b):
    B, T, _ = xin.shape
    H, N, C = RWKV_HEADS, RWKV_N, RWKV_W
    dtype = xin.dtype
    xin = xin.astype(F32)
    prev = jnp.pad(xin, [(0, 0), (1, 0), (0, 0)])[:, :-1]
    xm = xin + (prev - xin) * mu
    r, k, v, xw, xa, xg = _split(xm, (C, C, C, RWKV_DECAY_LORA, RWKV_A_LORA, RWKV_G_LORA))
    w_log = -jax.nn.softplus(-(w0 + jnp.tanh(xw) @ w2)) - 0.5
    decay = jnp.exp(-jnp.exp(w_log))
    a = jax.nn.sigmoid(a0 + xa @ a2)
    g = jax.nn.sigmoid(xg) @ g2

    def heads(t):
        return t.reshape(B, T, H, N)

    kk = l2norm(heads(k * k_k))
    k = k * (1.0 + (a - 1.0) * k_a)
    r_h, k_h, v_h, a_h = heads(r), heads(k), heads(v), heads(a)
    seq = tuple(jnp.moveaxis(t, 1, 0) for t in (r_h, heads(decay), k_h, v_h, -kk, kk * a_h))

    def step(S, inp):
        r_t, w_t, k_t, v_t, a_t, b_t = inp
        sa = jnp.einsum('bhij,bhj->bhi', S, a_t)
        S = S * w_t[:, :, None, :] + sa[..., None] * b_t[:, :, None, :] + v_t[..., None] * k_t[:, :, None, :]
        return S, jnp.einsum('bhij,bhj->bhi', S, r_t)

    _, y = lax.scan(step, jnp.zeros((B, H, N, N), F32), seq)
    y = jnp.moveaxis(y, 0, 1)
    mean = jnp.mean(y, axis=-1, keepdims=True)
    var = jnp.mean(jnp.square(y - mean), axis=-1, keepdims=True)
    y = ((y - mean) * lax.rsqrt(var + RWKV_LN_EPS)).reshape(B, T, C) * ln_w + ln_b
    y = y + (jnp.sum(r_h * k_h * r_k, axis=-1, keepdims=True) * v_h).reshape(B, T, C)
    return (y * g).astype(dtype)


def hybrid_mixer(h, w_in, gdn_p, mla_p, rwkv_p, w_branch, w_out, cos, sin):
    B, T, _ = h.shape
    proj = h @ w_in
    gq, gk, gv, gz, gb, ga, cq, ckv, kpe, rw, gate_logits = _split(proj, IN_SPLITS)
    y_a = gated_deltanet(gq, gk, gv, gz, gb, ga, *gdn_p)
    y_b = mla(cq, ckv, kpe, *mla_p, cos, sin)
    y_c = rwkv7_time_mix(rw, *rwkv_p)
    gate = jax.nn.sigmoid(gate_logits.astype(F32)).reshape(B, T, N_BRANCH, D_MODEL)
    merged = (gate[:, :, 0] * (y_a @ w_branch[0]).astype(F32)
              + gate[:, :, 1] * (y_b @ w_branch[1]).astype(F32)
              + gate[:, :, 2] * (y_c @ w_branch[2]).astype(F32))
    return (merged.astype(h.dtype) @ w_out).astype(h.dtype)


def hierarchical_moe(h, w_rg, b_rg, w_re, b_re, w_gate, w_up, w_down):
    B, T, D = h.shape
    n = B * T
    xt = h.reshape(n, D)
    g_logits = (xt @ w_rg + b_rg).astype(F32)
    g_top, g_idx = lax.top_k(g_logits, 1)
    p_group = jnp.exp(g_top - jax.nn.logsumexp(g_logits, axis=-1, keepdims=True))
    e_logits = (xt @ w_re + b_re).astype(F32).reshape(n, N_GROUPS, EXPERTS_PER_GROUP)
    e_in_group = e_logits[jnp.arange(n), g_idx[:, 0]]
    e_top, e_idx = lax.top_k(e_in_group, TOP_K)
    gate_w = jax.nn.softmax(e_top, axis=-1) * p_group
    expert = (g_idx * EXPERTS_PER_GROUP + e_idx).reshape(-1)
    n_assign = n * TOP_K
    order = jnp.argsort(expert)
    sorted_expert = expert[order]
    counts = jnp.bincount(expert, length=N_EXPERTS)
    starts = jnp.cumsum(counts) - counts
    padded = (counts + MOE_BLOCK - 1) // MOE_BLOCK * MOE_BLOCK
    padded_end = jnp.cumsum(padded)
    padded_start = padded_end - padded
    dest_sorted = (padded_start[sorted_expert] + jnp.arange(n_assign) - starts[sorted_expert]).astype(jnp.int32)
    n_rows = -(-n_assign // MOE_BLOCK) * MOE_BLOCK + N_EXPERTS * MOE_BLOCK
    n_blocks = n_rows // MOE_BLOCK
    row_token = jnp.full((n_rows,), n, jnp.int32).at[dest_sorted].set((order // TOP_K).astype(jnp.int32))
    block_expert = jnp.minimum(
        jnp.searchsorted(padded_end, jnp.arange(n_blocks) * MOE_BLOCK, side='right'), N_EXPERTS - 1)
    x_rows = jnp.concatenate([xt, jnp.zeros((1, D), xt.dtype)], axis=0)[row_token]
    x_rows = x_rows.reshape(n_blocks, MOE_BLOCK, D)

    def expert_block(args):
        xb, e = args
        return (jax.nn.silu(xb @ w_gate[e]) * (xb @ w_up[e])) @ w_down[e]

    y_rows = lax.map(expert_block, (x_rows, block_expert)).reshape(n_rows, D)
    dest = jnp.zeros((n_assign,), jnp.int32).at[order].set(dest_sorted)
    y = y_rows[dest].reshape(n, TOP_K, D).astype(F32)
    out = jnp.sum(gate_w[..., None] * y, axis=1)
    return out.reshape(B, T, D).astype(h.dtype)


def setup_inputs(seed: int = 0) -> dict:
    key = jax.random.key(seed)
    ks = iter(jax.random.split(key, 48))
    L, D = DEPTH, D_MODEL

    def nrm(shape, scale):
        return jax.random.normal(next(ks), shape, F32) * scale

    def gain(shape):
        return 1.0 + 0.02 * jax.random.normal(next(ks), shape, F32)

    def unif(shape, lo, hi):
        return jax.random.uniform(next(ks), shape, F32, lo, hi)

    dt = jnp.exp(unif((L, GDN_HEADS), math.log(1e-3), math.log(1e-1)))
    return {
        'x': nrm((BATCH, SEQ, D), 1.0),
        'meta_tokens': nrm((N_META, D), 1.0),
        'norm_mix': gain((L, D)),
        'w_in': nrm((L, D, D_IN), D ** -0.5),
        'gdn_conv': nrm((L, GDN_CONV, 2 * GDN_QK + GDN_VW), GDN_CONV ** -0.5),
        'gdn_a_log': jnp.log(unif((L, GDN_HEADS), 1.0, 16.0)),
        'gdn_dt_bias': dt + jnp.log(-jnp.expm1(-dt)),
        'gdn_norm': gain((L, GDN_DV)),
        'mla_q_norm': gain((L, MLA_Q_LORA)),
        'mla_w_q_up': nrm((L, MLA_Q_LORA, MLA_HEADS * (MLA_D_NOPE + MLA_D_ROPE)), MLA_Q_LORA ** -0.5),
        'mla_kv_norm': gain((L, MLA_KV_LORA)),
        'mla_w_kv_up': nrm((L, MLA_KV_LORA, MLA_HEADS * (MLA_D_NOPE + MLA_DV)), MLA_KV_LORA ** -0.5),
        'rwkv_mu': unif((L, RWKV_IN), 0.0, 1.0),
        'rwkv_w0': unif((L, RWKV_W), -6.0, 1.0),
        'rwkv_w2': nrm((L, RWKV_DECAY_LORA, RWKV_W), 0.1),
        'rwkv_a0': nrm((L, RWKV_W), 0.1),
        'rwkv_a2': nrm((L, RWKV_A_LORA, RWKV_W), 0.1),
        'rwkv_g2': nrm((L, RWKV_G_LORA, RWKV_W), RWKV_G_LORA ** -0.5),
        'rwkv_k_k': 0.85 + nrm((L, RWKV_W), 0.02),
        'rwkv_k_a': 1.0 + nrm((L, RWKV_W), 0.02),
        'rwkv_r_k': nrm((L, RWKV_HEADS, RWKV_N), 0.1),
        'rwkv_ln_w': gain((L, RWKV_W)),
        'rwkv_ln_b': nrm((L, RWKV_W), 0.01),
        'w_branch': nrm((L, N_BRANCH, BRANCH_W, D), BRANCH_W ** -0.5),
        'w_out': nrm((L, D, D), D ** -0.5),
        'norm_ffn': gain((L, D)),
        'w_router_group': nrm((L, D, N_GROUPS), D ** -0.5),
        'b_router_group': nrm((L, N_GROUPS), 0.01),
        'w_router_expert': nrm((L, D, N_EXPERTS), D ** -0.5),
        'b_router_expert': nrm((L, N_EXPERTS), 0.01),
        'w_exp_gate': nrm((L, N_EXPERTS, D, D_EXPERT), D ** -0.5),
        'w_exp_up': nrm((L, N_EXPERTS, D, D_EXPERT), D ** -0.5),
        'w_exp_down': nrm((L, N_EXPERTS, D_EXPERT, D), D_EXPERT ** -0.5),
        'norm_final': gain((D,)),
    }


def reference(x, meta_tokens, norm_mix, w_in, gdn_conv, gdn_a_log, gdn_dt_bias, gdn_norm,
              mla_q_norm, mla_w_q_up, mla_kv_norm, mla_w_kv_up,
              rwkv_mu, rwkv_w0, rwkv_w2, rwkv_a0, rwkv_a2, rwkv_g2, rwkv_k_k, rwkv_k_a,
              rwkv_r_k, rwkv_ln_w, rwkv_ln_b, w_branch, w_out, norm_ffn,
              w_router_group, b_router_group, w_router_expert, b_router_expert,
              w_exp_gate, w_exp_up, w_exp_down, norm_final):
    B = x.shape[0]
    meta = jnp.broadcast_to(meta_tokens[None].astype(x.dtype), (B, N_META, D_MODEL))
    h = jnp.concatenate([meta, x], axis=1)
    T = h.shape[1]
    pos = jnp.arange(T, dtype=F32)
    inv_freq = ROPE_THETA ** (-jnp.arange(0, MLA_D_ROPE, 2, dtype=F32) / MLA_D_ROPE)
    ang = pos[:, None] * inv_freq[None, :]
    cos, sin = jnp.cos(ang)[:, None, :], jnp.sin(ang)[:, None, :]
    for l in range(DEPTH):
        h = h + hybrid_mixer(
            rmsnorm(h, norm_mix[l]), w_in[l],
            (gdn_conv[l], gdn_a_log[l], gdn_dt_bias[l], gdn_norm[l]),
            (mla_q_norm[l], mla_w_q_up[l], mla_kv_norm[l], mla_w_kv_up[l]),
            (rwkv_mu[l], rwkv_w0[l], rwkv_w2[l], rwkv_a0[l], rwkv_a2[l], rwkv_g2[l],
             rwkv_k_k[l], rwkv_k_a[l], rwkv_r_k[l], rwkv_ln_w[l], rwkv_ln_b[l]),
            w_branch[l], w_out[l], cos, sin)
        h = h + hierarchical_moe(
            rmsnorm(h, norm_ffn[l]), w_router_group[l], b_router_group[l],
            w_router_expert[l], b_router_expert[l], w_exp_gate[l], w_exp_up[l], w_exp_down[l])
    return rmsnorm(h, norm_final)[:, N_META:]
```

```python
import functools
import math
from typing import NamedTuple

import jax
import jax.numpy as jnp
from jax import lax
from jax.experimental import pallas as pl
from jax.experimental.pallas import tpu as pltpu

F32 = jnp.float32
BF16 = jnp.bfloat16

D_MODEL = 2048
N_META = 16
NORM_EPS = 1e-6
L2_EPS = 1e-6

GDN_HEADS = 8
GDN_DK = 128
GDN_DV = 128
GDN_CONV = 4
GDN_QK = GDN_HEADS * GDN_DK
GDN_VW = GDN_HEADS * GDN_DV

MLA_HEADS = 8
MLA_Q_LORA = 768
MLA_KV_LORA = 512
MLA_D_NOPE = 128
MLA_D_ROPE = 64
MLA_DV = 128
MLA_DQK = 256
ROPE_THETA = 10000.0

RWKV_HEADS = 16
RWKV_N = 64
RWKV_W = RWKV_HEADS * RWKV_N
RWKV_DECAY_LORA = 64
RWKV_A_LORA = 64
RWKV_G_LORA = 160
RWKV_LORA = RWKV_DECAY_LORA + RWKV_A_LORA + RWKV_G_LORA
RWKV_LORA_PAD = 384
RWKV_LN_EPS = 64e-5
RWKV_IN = 3 * RWKV_W + RWKV_LORA
RWKV_IN_PAD = 3 * RWKV_W + RWKV_LORA_PAD

N_BRANCH = 3
BRANCH_W = 1024

N_GROUPS = 4
EXPERTS_PER_GROUP = 8
N_EXPERTS = N_GROUPS * EXPERTS_PER_GROUP
TOP_K = 2
D_EXPERT = 512

IN_SPLITS = (GDN_QK, GDN_QK, GDN_VW, GDN_VW, GDN_HEADS, GDN_HEADS,
             MLA_Q_LORA, MLA_KV_LORA, MLA_D_ROPE, RWKV_IN, N_BRANCH * D_MODEL)

LANES = 128
SUBLANES = 8
CHUNK = 64
ATTN_TILE = 256
MOE_TILE = 256
COMBINE_TILE = 128
VMEM_LIMIT = 48 * 1024 * 1024

NT_DIMS = (((1,), (1,)), ((), ()))
TN_DIMS = (((0,), (0,)), ((), ()))


class Layout(NamedTuple):
    batch: int
    t: int
    front: int
    tp: int

    @property
    def rows(self):
        return self.batch * self.tp


def make_layout(batch, seq):
    t = N_META + seq
    front = (-N_META) % CHUNK
    tp = -(-(front + t) // ATTN_TILE) * ATTN_TILE
    return Layout(batch, t, front, tp)


def _tile(n, pref, align):
    best = None
    for cand in range(align, min(n, pref) + 1, align):
        if n % cand == 0:
            best = cand
    if best is None:
        raise ValueError(f"no tile for {n}")
    return best


def _params(*sem):
    return pltpu.CompilerParams(dimension_semantics=sem, vmem_limit_bytes=VMEM_LIMIT)


def _valid_rows(p0, n, lay):
    p = p0 + lax.broadcasted_iota(jnp.int32, (n, 1), 0)
    p = jnp.where(p >= lay.tp, p - lay.tp, p)
    return (p >= lay.front) & (p < lay.front + lay.t)


def _split_bf16(a):
    hi = a.astype(BF16)
    lo = (a - hi.astype(F32)).astype(BF16)
    return hi, lo


def _dot(a, b, dims=None):
    a = a.astype(BF16)
    b = b.astype(BF16)
    if dims is None:
        return jnp.dot(a, b, preferred_element_type=F32)
    return lax.dot_general(a, b, dims, preferred_element_type=F32)


def _dot3(a, b):
    ah, al = _split_bf16(a)
    bh, bl = _split_bf16(b)
    return (jnp.dot(ah, bh, preferred_element_type=F32)
            + jnp.dot(al, bh, preferred_element_type=F32)
            + jnp.dot(ah, bl, preferred_element_type=F32))


def _sigmoid(x):
    return 1.0 / (1.0 + jnp.exp(-x))


def _softplus(x):
    return jnp.maximum(x, 0.0) + jnp.log(1.0 + jnp.exp(-jnp.abs(x)))


def _unit_lower_inverse(a, row, col):
    blk16 = (row >> 4) == (col >> 4)
    blk32 = (row >> 5) == (col >> 5)
    n = jnp.where(blk16, -a, 0.0)
    t = jnp.where(row == col, 1.0, 0.0) + n
    p = n
    for _ in range(3):
        p = _dot3(p, p)
        t = t + _dot3(t, p)
    e1 = jnp.where(blk32 & jnp.logical_not(blk16), a, 0.0)
    t = t - _dot3(_dot3(t, e1), t)
    e2 = jnp.where(blk32, 0.0, a)
    t = t - _dot3(_dot3(t, e2), t)
    return t


def _col_to_row(col_vec, eye):
    return jnp.sum(jnp.where(eye, col_vec, 0.0), axis=0, keepdims=True)


def _matmul_kernel(*refs, tm, norm, residual, lay):
    it = iter(refs)
    a_ref = next(it)
    b_ref = next(it)
    nw_ref = next(it) if norm else None
    r_ref = next(it) if residual else None
    o_ref = next(it)
    a_scr = next(it)
    i = pl.program_id(0)

    @pl.when(pl.program_id(1) == 0)
    def _():
        a = a_ref[...].astype(F32)
        if norm:
            ms = jnp.mean(a * a, axis=-1, keepdims=True)
            a = a * lax.rsqrt(ms + NORM_EPS) * nw_ref[...]
        if lay is not None:
            a = jnp.where(_valid_rows(lax.rem(i * tm, lay.tp), tm, lay), a, 0.0)
        a_scr[...] = a.astype(BF16)

    acc = jnp.dot(a_scr[...], b_ref[...], preferred_element_type=F32)
    if residual:
        acc = acc + r_ref[...]
    o_ref[...] = acc.astype(o_ref.dtype)


def _matmul(a, b, *, norm_w=None, residual=None, lay=None, out_dtype=F32, name="matmul"):
    m, k = a.shape
    n = b.shape[1]
    tm = _tile(m, 512, SUBLANES)
    tn = _tile(n, 512, LANES)
    in_specs = [pl.BlockSpec((tm, k), lambda i, j: (i, 0)),
                pl.BlockSpec((k, tn), lambda i, j: (0, j))]
    args = [a, b]
    if norm_w is not None:
        in_specs.append(pl.BlockSpec((1, k), lambda i, j: (0, 0)))
        args.append(norm_w.reshape(1, k).astype(F32))
    if residual is not None:
        in_specs.append(pl.BlockSpec((tm, tn), lambda i, j: (i, j)))
        args.append(residual)
    return pl.pallas_call(
        functools.partial(_matmul_kernel, tm=tm, norm=norm_w is not None,
                          residual=residual is not None, lay=lay),
        out_shape=jax.ShapeDtypeStruct((m, n), out_dtype),
        grid=(m // tm, n // tn),
        in_specs=in_specs,
        out_specs=pl.BlockSpec((tm, tn), lambda i, j: (i, j)),
        scratch_shapes=[pltpu.VMEM((tm, k), BF16)],
        compiler_params=_params("arbitrary", "arbitrary"),
        name=name,
    )(*args)


def _gdn_kernel(qkv_ref, z_ref, ba_ref, cw_ref, alog_ref, dtb_ref, nw_ref, o_ref,
                xbuf, s_scr, *, lay):
    c = pl.program_id(1)
    C = CHUNK
    width = 2 * GDN_QK + GDN_VW

    @pl.when(c == 0)
    def _():
        xbuf[0:SUBLANES, :] = jnp.zeros((SUBLANES, width), F32)
        s_scr[...] = jnp.zeros_like(s_scr)

    x = qkv_ref[...]
    xbuf[SUBLANES:SUBLANES + C, :] = x
    cw = cw_ref[...]
    y = x * cw[3:4, :]
    for d in range(1, GDN_CONV):
        y = y + xbuf[SUBLANES - d:SUBLANES - d + C, :] * cw[3 - d:4 - d, :]
    xbuf[0:SUBLANES, :] = x[C - SUBLANES:C, :]
    y = y * _sigmoid(y)

    valid = _valid_rows(c * C, C, lay)
    ba = ba_ref[...]
    beta_all = _sigmoid(ba)
    g_all = -jnp.exp(alog_ref[...]) * _softplus(ba + dtb_ref[...])
    g_all = jnp.where(valid, g_all, 0.0)

    row = lax.broadcasted_iota(jnp.int32, (C, C), 0)
    col = lax.broadcasted_iota(jnp.int32, (C, C), 1)
    eye = row == col
    causal = row >= col
    strict = row > col
    gc_all = _dot3(jnp.where(causal, 1.0, 0.0), g_all)

    z = z_ref[...]
    nw = nw_ref[...]
    for h in range(GDN_HEADS):
        q = y[:, h * GDN_DK:(h + 1) * GDN_DK]
        k = y[:, GDN_QK + h * GDN_DK:GDN_QK + (h + 1) * GDN_DK]
        v = y[:, 2 * GDN_QK + h * GDN_DV:2 * GDN_QK + (h + 1) * GDN_DV]
        q = q * lax.rsqrt(jnp.sum(q * q, axis=-1, keepdims=True) + L2_EPS) * GDN_DK ** -0.5
        k = k * lax.rsqrt(jnp.sum(k * k, axis=-1, keepdims=True) + L2_EPS)
        beta = beta_all[:, h:h + 1]
        gc = gc_all[:, GDN_HEADS + h:GDN_HEADS + h + 1]
        gc_row = _col_to_row(gc, eye)
        gc_last = gc[C - 1:C, :]
        decay = jnp.where(causal, jnp.exp(jnp.where(causal, gc - gc_row, 0.0)), 0.0)
        egc = jnp.exp(gc)
        kb = k * beta
        a = jnp.where(strict, _dot(kb, k, NT_DIMS) * decay, 0.0)
        t_inv = _unit_lower_inverse(a, row, col)
        u = _dot(t_inv, v * beta)
        w = _dot(t_inv, kb * egc)
        attn = _dot(q, k, NT_DIMS) * decay
        q_dec = q * egc
        k_dec = k * jnp.exp(gc_last - gc)
        s = s_scr[h]
        v_new = u - _dot(w, s)
        o = _dot(q_dec, s) + _dot(attn, v_new)
        s_scr[h] = s * jnp.exp(gc_last) + _dot(k_dec, v_new, TN_DIMS)
        o = o * lax.rsqrt(jnp.mean(o * o, axis=-1, keepdims=True) + NORM_EPS) * nw
        zh = z[:, h * GDN_DV:(h + 1) * GDN_DV]
        o_ref[:, h * GDN_DV:(h + 1) * GDN_DV] = (o * (zh * _sigmoid(zh))).astype(o_ref.dtype)


def _gdn(pg, conv_w, a_log, dt_bias, norm_w, lay):
    C = CHUNK
    nc = lay.tp // C
    width = 2 * GDN_QK + GDN_VW
    pad = LANES - 2 * GDN_HEADS
    alog = jnp.pad(a_log.astype(F32), (GDN_HEADS, pad)).reshape(1, LANES)
    dtb = jnp.pad(dt_bias.astype(F32), (GDN_HEADS, pad)).reshape(1, LANES)
    rows = lambda b, c: b * nc + c
    return pl.pallas_call(
        functools.partial(_gdn_kernel, lay=lay),
        out_shape=jax.ShapeDtypeStruct((lay.rows, GDN_VW), BF16),
        grid=(lay.batch, nc),
        in_specs=[
            pl.BlockSpec((C, width), lambda b, c: (rows(b, c), 0)),
            pl.BlockSpec((C, GDN_VW), lambda b, c: (rows(b, c), width // GDN_VW)),
            pl.BlockSpec((C, LANES), lambda b, c: (rows(b, c), (width + GDN_VW) // LANES)),
            pl.BlockSpec((GDN_CONV, width), lambda b, c: (0, 0)),
            pl.BlockSpec((1, LANES), lambda b, c: (0, 0)),
            pl.BlockSpec((1, LANES), lambda b, c: (0, 0)),
            pl.BlockSpec((1, GDN_DV), lambda b, c: (0, 0)),
        ],
        out_specs=pl.BlockSpec((C, GDN_VW), lambda b, c: (rows(b, c), 0)),
        scratch_shapes=[pltpu.VMEM((C + SUBLANES, width), F32),
                        pltpu.VMEM((GDN_HEADS, GDN_DK, GDN_DV), F32)],
        compiler_params=_params("arbitrary", "arbitrary"),
        name="gdn",
    )(pg, pg, pg, conv_w.astype(F32), alog, dtb, norm_w.reshape(1, GDN_DV).astype(F32))


def _mla_prep_kernel(q_ref, kv_ref, kpe_ref, cos_ref, sin_ref, qo_ref, ko_ref, vo_ref):
    cos = cos_ref[...]
    sin = sin_ref[...]
    lane = lax.broadcasted_iota(jnp.int32, cos.shape, 1)
    half = MLA_D_ROPE // 2

    def rot(x):
        swapped = jnp.where(lane < half, pltpu.roll(x, LANES - half, 1), pltpu.roll(x, half, 1))
        return x * cos + swapped * sin

    scale = (MLA_D_NOPE + MLA_D_ROPE) ** -0.5
    k_pe = rot(kpe_ref[...]).astype(BF16)
    for h in range(MLA_HEADS):
        lo = h * MLA_DQK
        qo_ref[:, lo:lo + MLA_D_NOPE] = (q_ref[:, lo:lo + MLA_D_NOPE] * scale).astype(BF16)
        qo_ref[:, lo + MLA_D_NOPE:lo + MLA_DQK] = (
            rot(q_ref[:, lo + MLA_D_NOPE:lo + MLA_DQK]) * scale).astype(BF16)
        ko_ref[:, lo:lo + MLA_D_NOPE] = kv_ref[:, h * MLA_D_NOPE:(h + 1) * MLA_D_NOPE].astype(BF16)
        ko_ref[:, lo + MLA_D_NOPE:lo + MLA_DQK] = k_pe
    vo_ref[...] = kv_ref[:, MLA_HEADS * MLA_D_NOPE:].astype(BF16)


def _flash_kernel(q_ref, k_ref, v_ref, o_ref, m_scr, l_scr, acc_scr, *, lay):
    qi = pl.program_id(2)
    T = ATTN_TILE
    q = q_ref[...]
    m_scr[...] = jnp.full_like(m_scr, -jnp.inf)
    l_scr[...] = jnp.zeros_like(l_scr)
    acc_scr[...] = jnp.zeros_like(acc_scr)
    q_pos = qi * T + lax.broadcasted_iota(jnp.int32, (T, T), 0)
    k_off = lax.broadcasted_iota(jnp.int32, (T, T), 1)

    def body(ki, carry):
        start = pl.multiple_of(ki * T, T)
        k = k_ref[pl.ds(start, T), :]
        v = v_ref[pl.ds(start, T), :]
        s = lax.dot_general(q, k, NT_DIMS, preferred_element_type=F32)
        k_pos = ki * T + k_off
        ok = (k_pos <= q_pos) & ((k_pos >= lay.front) | (k_pos == q_pos))
        s = jnp.where(ok, s, -jnp.inf)
        m_prev = m_scr[...]
        m_new = jnp.maximum(m_prev, jnp.max(s, axis=-1, keepdims=True))
        p = jnp.exp(s - m_new)
        alpha = jnp.exp(m_prev - m_new)
        l_scr[...] = alpha * l_scr[...] + jnp.sum(p, axis=-1, keepdims=True)
        acc_scr[...] = alpha * acc_scr[...] + jnp.dot(p.astype(BF16), v, preferred_element_type=F32)
        m_scr[...] = m_new
        return carry

    lax.fori_loop(0, qi + 1, body, 0)
    o_ref[...] = (acc_scr[...] / l_scr[...]).astype(o_ref.dtype)


def _rope_tables(lay):
    pos = (jnp.arange(lay.tp) - lay.front).astype(F32)
    inv_freq = ROPE_THETA ** (-jnp.arange(0, MLA_D_ROPE, 2, dtype=F32) / MLA_D_ROPE)
    ang = pos[:, None] * inv_freq[None, :]
    cos, sin = jnp.cos(ang), jnp.sin(ang)
    zeros = jnp.zeros((lay.tp, LANES - MLA_D_ROPE), F32)
    return (jnp.concatenate([cos, cos, zeros], axis=1),
            jnp.concatenate([-sin, sin, zeros], axis=1))


def _mla(p_cq, p_ckv, p_kpe, q_norm, wq, kv_norm, wkv, tables, lay):
    qraw = _matmul(p_cq, wq, norm_w=q_norm, name="mla_q_up")
    kvraw = _matmul(p_ckv, wkv, norm_w=kv_norm, name="mla_kv_up")
    rows = lay.rows
    tm = ATTN_TILE
    nt = lay.tp // tm
    hq = MLA_HEADS * MLA_DQK
    hv = MLA_HEADS * MLA_DV
    cos, sin = tables
    q, k, v = pl.pallas_call(
        _mla_prep_kernel,
        out_shape=(jax.ShapeDtypeStruct((rows, hq), BF16),
                   jax.ShapeDtypeStruct((rows, hq), BF16),
                   jax.ShapeDtypeStruct((rows, hv), BF16)),
        grid=(rows // tm,),
        in_specs=[pl.BlockSpec((tm, hq), lambda i: (i, 0)),
                  pl.BlockSpec((tm, MLA_HEADS * (MLA_D_NOPE + MLA_DV)), lambda i: (i, 0)),
                  pl.BlockSpec((tm, LANES), lambda i: (i, 0)),
                  pl.BlockSpec((tm, LANES), lambda i: (i % nt, 0)),
                  pl.BlockSpec((tm, LANES), lambda i: (i % nt, 0))],
        out_specs=(pl.BlockSpec((tm, hq), lambda i: (i, 0)),
                   pl.BlockSpec((tm, hq), lambda i: (i, 0)),
                   pl.BlockSpec((tm, hv), lambda i: (i, 0))),
        compiler_params=_params("arbitrary"),
        name="mla_prep",
    )(qraw, kvraw, p_kpe, cos, sin)
    return pl.pallas_call(
        functools.partial(_flash_kernel, lay=lay),
        out_shape=jax.ShapeDtypeStruct((rows, hv), BF16),
        grid=(lay.batch, MLA_HEADS, nt),
        in_specs=[pl.BlockSpec((tm, MLA_DQK), lambda b, h, i: (b * nt + i, h)),
                  pl.BlockSpec((lay.tp, MLA_DQK), lambda b, h, i: (b, h)),
                  pl.BlockSpec((lay.tp, MLA_DV), lambda b, h, i: (b, h))],
        out_specs=pl.BlockSpec((tm, MLA_DV), lambda b, h, i: (b * nt + i, h)),
        scratch_shapes=[pltpu.VMEM((tm, 1), F32), pltpu.VMEM((tm, 1), F32),
                        pltpu.VMEM((tm, MLA_DV), F32)],
        compiler_params=_params("arbitrary", "arbitrary", "arbitrary"),
        name="mla_attention",
    )(q, k, v)


def _rwkv_kernel(x_ref, mu_ref, wl_ref, w0_ref, a0_ref, kk_ref, ka_ref, rk_ref, lnw_ref, lnb_ref,
                 o_ref, xbuf, s_scr, *, lay):
    c = pl.program_id(1)
    C = CHUNK
    W = RWKV_W
    N = RWKV_N

    @pl.when(c == 0)
    def _():
        xbuf[0:SUBLANES, :] = jnp.zeros((SUBLANES, RWKV_IN_PAD), F32)
        s_scr[...] = jnp.zeros_like(s_scr)

    x = x_ref[...]
    xbuf[SUBLANES:SUBLANES + C, :] = x
    prev = xbuf[SUBLANES - 1:SUBLANES - 1 + C, :]
    xbuf[0:SUBLANES, :] = x[C - SUBLANES:C, :]
    xm = x + (prev - x) * mu_ref[...]
    xm = jnp.where(_valid_rows(c * C, C, lay), xm, 0.0)
    r = xm[:, 0:W]
    k = xm[:, W:2 * W]
    v = xm[:, 2 * W:3 * W]
    lin = xm[:, 3 * W:]
    lane = lax.broadcasted_iota(jnp.int32, lin.shape, 1)
    d1 = RWKV_DECAY_LORA
    d2 = d1 + RWKV_A_LORA
    lin = jnp.where(lane < d1, jnp.tanh(lin),
                    jnp.where(lane < d2, lin, jnp.where(lane < RWKV_LORA, _sigmoid(lin), 0.0)))
    lo = _dot(lin, wl_ref[...])
    w_log = -_softplus(-(w0_ref[...] + lo[:, 0:W])) - 0.5
    lw = -jnp.exp(w_log)
    a = _sigmoid(a0_ref[...] + lo[:, W:2 * W])
    g = lo[:, 2 * W:3 * W]
    kk_raw = k * kk_ref[...]
    k2 = k * (1.0 + (a - 1.0) * ka_ref[...])
    rkr = r * k2 * rk_ref[...]

    row = lax.broadcasted_iota(jnp.int32, (C, C), 0)
    col = lax.broadcasted_iota(jnp.int32, (C, C), 1)
    causal = row >= col
    strict = row > col
    cum = _dot3(jnp.where(causal, 1.0, 0.0), lw)
    c_last = cum[C - 1:C, :]
    e_prev = jnp.exp(cum - lw)
    e_neg = jnp.exp(-cum)
    e_pos = jnp.exp(cum)
    e_rest = jnp.exp(c_last - cum)
    e_last = jnp.exp(c_last)
    lnw = lnw_ref[...]
    lnb = lnb_ref[...]

    for h in range(RWKV_HEADS):
        sl = slice(h * N, (h + 1) * N)
        kk = kk_raw[:, sl]
        kk = kk * lax.rsqrt(jnp.sum(kk * kk, axis=-1, keepdims=True) + L2_EPS)
        vh = v[:, sl]
        b_vec = kk * a[:, sl]
        a_hat = -kk * e_prev[:, sl]
        b_hat = b_vec * e_neg[:, sl]
        k_hat = k2[:, sl] * e_neg[:, sl]
        r_hat = r[:, sl] * e_pos[:, sl]
        a_ab = jnp.where(strict, _dot(a_hat, b_hat, NT_DIMS), 0.0)
        a_ak = jnp.where(strict, _dot(a_hat, k_hat, NT_DIMS), 0.0)
        r_b = jnp.where(causal, _dot(r_hat, b_hat, NT_DIMS), 0.0)
        r_k = jnp.where(causal, _dot(r_hat, k_hat, NT_DIMS), 0.0)
        t_inv = _unit_lower_inverse(-a_ab, row, col)
        s = s_scr[h]
        u = _dot(t_inv, _dot(a_hat, s, NT_DIMS) + _dot(a_ak, vh))
        yh = _dot(r_hat, s, NT_DIMS) + _dot(r_b, u) + _dot(r_k, vh)
        s_scr[h] = (s * e_last[:, sl]
                    + _dot(u, b_vec * e_rest[:, sl], TN_DIMS)
                    + _dot(vh, k2[:, sl] * e_rest[:, sl], TN_DIMS))
        mean = jnp.mean(yh, axis=-1, keepdims=True)
        cen = yh - mean
        var = jnp.mean(cen * cen, axis=-1, keepdims=True)
        yn = cen * lax.rsqrt(var + RWKV_LN_EPS) * lnw[:, sl] + lnb[:, sl]
        bonus = jnp.sum(rkr[:, sl], axis=-1, keepdims=True) * vh
        o_ref[:, sl] = ((yn + bonus) * g[:, sl]).astype(o_ref.dtype)


def _rwkv(p_rw, mu, w_lora, w0, a0, k_k, k_a, r_k, ln_w, ln_b, lay):
    C = CHUNK
    nc = lay.tp // C
    vec = lambda t: t.reshape(1, RWKV_W).astype(F32)
    rows = lambda b, c: (b * nc + c, 0)
    const = lambda b, c: (0, 0)
    return pl.pallas_call(
        functools.partial(_rwkv_kernel, lay=lay),
        out_shape=jax.ShapeDtypeStruct((lay.rows, RWKV_W), BF16),
        grid=(lay.batch, nc),
        in_specs=[pl.BlockSpec((C, RWKV_IN_PAD), rows),
                  pl.BlockSpec((1, RWKV_IN_PAD), const),
                  pl.BlockSpec((RWKV_LORA_PAD, 3 * RWKV_W), const)]
                 + [pl.BlockSpec((1, RWKV_W), const)] * 7,
        out_specs=pl.BlockSpec((C, RWKV_W), rows),
        scratch_shapes=[pltpu.VMEM((C + SUBLANES, RWKV_IN_PAD), F32),
                        pltpu.VMEM((RWKV_HEADS, RWKV_N, RWKV_N), F32)],
        compiler_params=_params("arbitrary", "arbitrary"),
        name="rwkv7",
    )(p_rw, jnp.pad(mu.astype(F32), (0, RWKV_IN_PAD - RWKV_IN)).reshape(1, RWKV_IN_PAD), w_lora,
      vec(w0), vec(a0), vec(k_k), vec(k_a), vec(r_k), vec(ln_w), vec(ln_b))


def _rwkv_lora_weight(w2, a2, g2):
    w = jnp.zeros((RWKV_LORA_PAD, 3 * RWKV_W), F32)
    d1 = RWKV_DECAY_LORA
    d2 = d1 + RWKV_A_LORA
    w = w.at[0:d1, 0:RWKV_W].set(w2)
    w = w.at[d1:d2, RWKV_W:2 * RWKV_W].set(a2)
    w = w.at[d2:RWKV_LORA, 2 * RWKV_W:].set(g2)
    return w.astype(BF16)


def _merge_kernel(ya_ref, yb_ref, yc_ref, wb_ref, g0_ref, g1_ref, g2_ref, o_ref):
    acc = _sigmoid(g0_ref[...]) * jnp.dot(ya_ref[...], wb_ref[0], preferred_element_type=F32)
    acc += _sigmoid(g1_ref[...]) * jnp.dot(yb_ref[...], wb_ref[1], preferred_element_type=F32)
    acc += _sigmoid(g2_ref[...]) * jnp.dot(yc_ref[...], wb_ref[2], preferred_element_type=F32)
    o_ref[...] = acc.astype(o_ref.dtype)


def _merge(ya, yb, yc, w_branch, gates):
    rows = ya.shape[0]
    tm = _tile(rows, 512, SUBLANES)
    tn = 512
    nj = D_MODEL // tn
    y_spec = pl.BlockSpec((tm, BRANCH_W), lambda i, j: (i, 0))
    gate = lambda br: pl.BlockSpec((tm, tn), lambda i, j: (i, br * nj + j))
    return pl.pallas_call(
        _merge_kernel,
        out_shape=jax.ShapeDtypeStruct((rows, D_MODEL), BF16),
        grid=(rows // tm, nj),
        in_specs=[y_spec, y_spec, y_spec,
                  pl.BlockSpec((N_BRANCH, BRANCH_W, tn), lambda i, j: (0, 0, j)),
                  gate(0), gate(1), gate(2)],
        out_specs=pl.BlockSpec((tm, tn), lambda i, j: (i, j)),
        compiler_params=_params("arbitrary", "arbitrary"),
        name="gated_merge",
    )(ya, yb, yc, w_branch, gates, gates, gates)


def _router_kernel(h_ref, nw_ref, wr_ref, br_ref, xt_ref, idx_ref, gw_ref, *, tm, lay):
    i = pl.program_id(0)
    x = h_ref[...]
    x = x * lax.rsqrt(jnp.mean(x * x, axis=-1, keepdims=True) + NORM_EPS) * nw_ref[...]
    x = jnp.where(_valid_rows(lax.rem(i * tm, lay.tp), tm, lay), x, 0.0)
    xt_ref[...] = x
    logits = _dot3(x, wr_ref[...]) + br_ref[...]
    lane = lax.broadcasted_iota(jnp.int32, logits.shape, 1)
    neg = -jnp.inf

    def top1(vals):
        m = jnp.max(vals, axis=-1, keepdims=True)
        idx = jnp.min(jnp.where(vals == m, lane, LANES), axis=-1, keepdims=True)
        return m, idx

    g_mask = lane < N_GROUPS
    g_top, g_idx = top1(jnp.where(g_mask, logits, neg))
    p_group = 1.0 / jnp.sum(jnp.where(g_mask, jnp.exp(logits - g_top), 0.0), axis=-1, keepdims=True)
    e_lo = N_GROUPS + g_idx * EXPERTS_PER_GROUP
    e_vals = jnp.where((lane >= e_lo) & (lane < e_lo + EXPERTS_PER_GROUP), logits, neg)
    e1, i1 = top1(e_vals)
    e2, i2 = top1(jnp.where(lane == i1, neg, e_vals))
    t = jnp.exp(e2 - e1)
    w1 = p_group / (1.0 + t)
    w2 = p_group * t / (1.0 + t)
    idx_ref[...] = jnp.where(lane == 0, i1 - N_GROUPS, jnp.where(lane == 1, i2 - N_GROUPS, 0))
    gw_ref[...] = jnp.where(lane == 0, w1, jnp.where(lane == 1, w2, 0.0))


def _router(h, norm_w, w_r, b_r, lay):
    rows = h.shape[0]
    tm = _tile(rows, 512, SUBLANES)
    row_spec = lambda w: pl.BlockSpec((tm, w), lambda i: (i, 0))
    return pl.pallas_call(
        functools.partial(_router_kernel, tm=tm, lay=lay),
        out_shape=(jax.ShapeDtypeStruct((rows, D_MODEL), F32),
                   jax.ShapeDtypeStruct((rows, LANES), jnp.int32),
                   jax.ShapeDtypeStruct((rows, LANES), F32)),
        grid=(rows // tm,),
        in_specs=[row_spec(D_MODEL),
                  pl.BlockSpec((1, D_MODEL), lambda i: (0, 0)),
                  pl.BlockSpec((D_MODEL, LANES), lambda i: (0, 0)),
                  pl.BlockSpec((1, LANES), lambda i: (0, 0))],
        out_specs=(row_spec(D_MODEL), row_spec(LANES), row_spec(LANES)),
        compiler_params=_params("arbitrary"),
        name="moe_router",
    )(h, norm_w.reshape(1, D_MODEL).astype(F32), w_r, b_r)


def _row_gather(idx_ref, base, n, src_hbm, dst, sem):
    def body(r, carry):
        tok = idx_ref[base + r]
        pltpu.make_async_copy(src_hbm.at[pl.ds(tok, 1), :], dst.at[pl.ds(r, 1), :], sem).start()
        return carry
    lax.fori_loop(0, n, body, 0)


def _row_gather_wait(n, src_hbm, dst, sem):
    def body(r, carry):
        pltpu.make_async_copy(src_hbm.at[pl.ds(0, 1), :], dst.at[pl.ds(r, 1), :], sem).wait()
        return carry
    lax.fori_loop(0, n, body, 0)


def _expert_kernel(be_ref, tok_ref, nused_ref, xt_hbm, wgu_ref, wd_ref, o_ref, xbuf, sem):
    i = pl.program_id(0)
    nb = pl.num_programs(0)
    tb = MOE_TILE
    slot = i % 2
    used = nused_ref[0]

    @pl.when((i == 0) & (used > 0))
    def _():
        _row_gather(tok_ref, 0, tb, xt_hbm, xbuf.at[0], sem.at[0])

    @pl.when((i + 1 < nb) & (i + 1 < used))
    def _():
        _row_gather(tok_ref, (i + 1) * tb, tb, xt_hbm, xbuf.at[1 - slot], sem.at[1 - slot])

    @pl.when(i < used)
    def _():
        _row_gather_wait(tb, xt_hbm, xbuf.at[slot], sem.at[slot])
        x = xbuf[slot].astype(BF16)
        gu = jnp.dot(x, wgu_ref[0], preferred_element_type=F32)
        gate = gu[:, :D_EXPERT]
        act = (gate * _sigmoid(gate) * gu[:, D_EXPERT:]).astype(BF16)
        o_ref[...] = jnp.dot(act, wd_ref[0], preferred_element_type=F32)

    @pl.when(i >= used)
    def _():
        o_ref[...] = jnp.zeros_like(o_ref)


def _experts(xt, block_expert, row_token, n_used, w_gu, w_down):
    n_rows = row_token.shape[0]
    nb = n_rows // MOE_TILE
    grid_spec = pltpu.PrefetchScalarGridSpec(
        num_scalar_prefetch=3,
        grid=(nb,),
        in_specs=[pl.BlockSpec(memory_space=pl.ANY),
                  pl.BlockSpec((1, D_MODEL, 2 * D_EXPERT), lambda i, be, tok, nu: (be[i], 0, 0)),
                  pl.BlockSpec((1, D_EXPERT, D_MODEL), lambda i, be, tok, nu: (be[i], 0, 0))],
        out_specs=pl.BlockSpec((MOE_TILE, D_MODEL), lambda i, be, tok, nu: (i, 0)),
        scratch_shapes=[pltpu.VMEM((2, MOE_TILE, D_MODEL), F32),
                        pltpu.SemaphoreType.DMA((2,))],
    )
    return pl.pallas_call(
        _expert_kernel,
        out_shape=jax.ShapeDtypeStruct((n_rows, D_MODEL), F32),
        grid_spec=grid_spec,
        compiler_params=_params("arbitrary"),
        name="moe_experts",
    )(block_expert, row_token, n_used, xt, w_gu, w_down)


def _combine_kernel(dest0_ref, dest1_ref, y_hbm, h_ref, gw_ref, o_ref, buf0, buf1, sem):
    i = pl.program_id(0)
    nb = pl.num_programs(0)
    tc = COMBINE_TILE
    slot = i % 2

    def start(step, s):
        _row_gather(dest0_ref, step * tc, tc, y_hbm, buf0.at[s], sem.at[0, s])
        _row_gather(dest1_ref, step * tc, tc, y_hbm, buf1.at[s], sem.at[1, s])

    @pl.when(i == 0)
    def _():
        start(0, 0)

    @pl.when(i + 1 < nb)
    def _():
        start(i + 1, 1 - slot)

    _row_gather_wait(tc, y_hbm, buf0.at[slot], sem.at[0, slot])
    _row_gather_wait(tc, y_hbm, buf1.at[slot], sem.at[1, slot])
    gw = gw_ref[...]
    o_ref[...] = h_ref[...] + gw[:, 0:1] * buf0[slot] + gw[:, 1:2] * buf1[slot]


def _combine(h, y_rows, dest0, dest1, gw):
    rows = h.shape[0]
    tc = COMBINE_TILE
    grid_spec = pltpu.PrefetchScalarGridSpec(
        num_scalar_prefetch=2,
        grid=(rows // tc,),
        in_specs=[pl.BlockSpec(memory_space=pl.ANY),
                  pl.BlockSpec((tc, D_MODEL), lambda i, d0, d1: (i, 0)),
                  pl.BlockSpec((tc, LANES), lambda i, d0, d1: (i, 0))],
        out_specs=pl.BlockSpec((tc, D_MODEL), lambda i, d0, d1: (i, 0)),
        scratch_shapes=[pltpu.VMEM((2, tc, D_MODEL), F32),
                        pltpu.VMEM((2, tc, D_MODEL), F32),
                        pltpu.SemaphoreType.DMA((2, 2))],
    )
    return pl.pallas_call(
        _combine_kernel,
        out_shape=jax.ShapeDtypeStruct((rows, D_MODEL), F32),
        grid_spec=grid_spec,
        compiler_params=_params("arbitrary"),
        name="moe_combine",
    )(dest0, dest1, y_rows, h, gw)


def _moe(h, norm_w, w_r, b_r, w_gu, w_down, lay):
    rows = h.shape[0]
    xt, idx, gw = _router(h, norm_w, w_r, b_r, lay)
    expert = idx[:, :TOP_K].reshape(-1)
    n_assign = rows * TOP_K
    onehot = (expert[:, None] == jnp.arange(N_EXPERTS, dtype=jnp.int32)[None, :]).astype(jnp.int32)
    csum = jnp.cumsum(onehot, axis=0)
    counts = csum[-1]
    rank = jnp.sum(csum * onehot, axis=1) - 1
    padded = (counts + MOE_TILE - 1) // MOE_TILE * MOE_TILE
    padded_end = jnp.cumsum(padded)
    padded_start = padded_end - padded
    dest = (padded_start[expert] + rank).astype(jnp.int32)
    n_rows = (-(-n_assign // MOE_TILE) + N_EXPERTS) * MOE_TILE
    nb = n_rows // MOE_TILE
    row_token = jnp.zeros((n_rows,), jnp.int32).at[dest].set(
        jnp.arange(n_assign, dtype=jnp.int32) // TOP_K, unique_indices=True)
    block_expert = jnp.minimum(
        jnp.searchsorted(padded_end, jnp.arange(nb, dtype=jnp.int32) * MOE_TILE, side="right"),
        N_EXPERTS - 1).astype(jnp.int32)
    n_used = (padded_end[-1:] // MOE_TILE).astype(jnp.int32)
    y_rows = _experts(xt, block_expert, row_token, n_used, w_gu, w_down)
    dest2 = dest.reshape(rows, TOP_K)
    return _combine(h, y_rows, dest2[:, 0], dest2[:, 1], gw)


def _final_norm_kernel(h_ref, w_ref, o_ref):
    x = h_ref[...]
    o_ref[...] = x * lax.rsqrt(jnp.mean(x * x, axis=-1, keepdims=True) + NORM_EPS) * w_ref[...]


def _final_norm(h, w, lay, seq):
    first = lay.front + N_META
    tm = math.gcd(math.gcd(first, seq), lay.tp)
    per_b = lay.tp // tm
    nseq = seq // tm
    return pl.pallas_call(
        _final_norm_kernel,
        out_shape=jax.ShapeDtypeStruct((lay.batch * seq, D_MODEL), F32),
        grid=(lay.batch, nseq),
        in_specs=[pl.BlockSpec((tm, D_MODEL), lambda b, i: (b * per_b + first // tm + i, 0)),
                  pl.BlockSpec((1, D_MODEL), lambda b, i: (0, 0))],
        out_specs=pl.BlockSpec((tm, D_MODEL), lambda b, i: (b * nseq + i, 0)),
        compiler_params=_params("arbitrary", "arbitrary"),
        name="final_norm",
    )(h, w.reshape(1, D_MODEL).astype(F32))


def _pad_cols(w, width):
    return jnp.pad(w, ((0, 0), (0, width - w.shape[1])))


def _layer_weights(w_in, mla_w_q_up, mla_w_kv_up, w2, a2, g2, w_branch, w_out,
                   w_rg, b_rg, w_re, b_re, w_gate, w_up, w_down):
    offs = [0]
    for s in IN_SPLITS:
        offs.append(offs[-1] + s)
    seg = lambda i, j: w_in[:, offs[i]:offs[j]]
    w_gdn = jnp.concatenate([seg(0, 4), _pad_cols(seg(4, 6), LANES)], axis=1).astype(BF16)
    w_cq = seg(6, 7).astype(BF16)
    w_ckv = seg(7, 8).astype(BF16)
    w_kpe = _pad_cols(seg(8, 9), LANES).astype(BF16)
    w_rw = _pad_cols(seg(9, 10), RWKV_IN_PAD).astype(BF16)
    w_gates = seg(10, 11).astype(BF16)
    wq = mla_w_q_up.reshape(MLA_Q_LORA, MLA_HEADS, MLA_D_NOPE + MLA_D_ROPE)
    wq = jnp.pad(wq, ((0, 0), (0, 0), (0, MLA_DQK - MLA_D_NOPE - MLA_D_ROPE)))
    wq = wq.reshape(MLA_Q_LORA, MLA_HEADS * MLA_DQK).astype(BF16)
    wkv = mla_w_kv_up.reshape(MLA_KV_LORA, MLA_HEADS, MLA_D_NOPE + MLA_DV)
    wkv = jnp.concatenate([wkv[:, :, :MLA_D_NOPE].reshape(MLA_KV_LORA, -1),
                           wkv[:, :, MLA_D_NOPE:].reshape(MLA_KV_LORA, -1)], axis=1).astype(BF16)
    w_r = _pad_cols(jnp.concatenate([w_rg, w_re], axis=1), LANES).astype(F32)
    b_r = jnp.pad(jnp.concatenate([b_rg, b_re]), (0, LANES - N_GROUPS - N_EXPERTS)).reshape(1, LANES)
    return dict(
        w_gdn=w_gdn, w_cq=w_cq, w_ckv=w_ckv, w_kpe=w_kpe, w_rw=w_rw, w_gates=w_gates,
        wq=wq, wkv=wkv, w_lora=_rwkv_lora_weight(w2, a2, g2),
        w_branch=w_branch.astype(BF16), w_out=w_out.astype(BF16), w_r=w_r, b_r=b_r.astype(F32),
        w_gu=jnp.concatenate([w_gate, w_up], axis=2).astype(BF16), w_down=w_down.astype(BF16))


def _mixer(h, norm_w, lw, gdn_p, mla_p, rwkv_p, tables, lay):
    proj = lambda w, name: _matmul(h, w, norm_w=norm_w, lay=lay, name=name)
    ya = _gdn(proj(lw["w_gdn"], "proj_gdn"), *gdn_p, lay)
    yb = _mla(proj(lw["w_cq"], "proj_cq"), proj(lw["w_ckv"], "proj_ckv"), proj(lw["w_kpe"], "proj_kpe"),
              mla_p[0], lw["wq"], mla_p[1], lw["wkv"], tables, lay)
    yc = _rwkv(proj(lw["w_rw"], "proj_rwkv"), rwkv_p[0], lw["w_lora"], *rwkv_p[1:], lay)
    merged = _merge(ya, yb, yc, lw["w_branch"], proj(lw["w_gates"], "proj_gates"))
    return _matmul(merged, lw["w_out"], residual=h, name="mixer_out")


def kernel(x, meta_tokens, norm_mix, w_in, gdn_conv, gdn_a_log, gdn_dt_bias, gdn_norm, mla_q_norm, mla_w_q_up, mla_kv_norm, mla_w_kv_up, rwkv_mu, rwkv_w0, rwkv_w2, rwkv_a0, rwkv_a2, rwkv_g2, rwkv_k_k, rwkv_k_a, rwkv_r_k, rwkv_ln_w, rwkv_ln_b, w_branch, w_out, norm_ffn, w_router_group, b_router_group, w_router_expert, b_router_expert, w_exp_gate, w_exp_up, w_exp_down, norm_final):
    batch, seq, _ = x.shape
    depth = w_in.shape[0]
    lay = make_layout(batch, seq)
    meta = jnp.broadcast_to(meta_tokens[None].astype(x.dtype), (batch, N_META, D_MODEL))
    h = jnp.concatenate([jnp.zeros((batch, lay.front, D_MODEL), x.dtype), meta, x,
                         jnp.zeros((batch, lay.tp - lay.front - lay.t, D_MODEL), x.dtype)], axis=1)
    h = h.reshape(lay.rows, D_MODEL)
    tables = _rope_tables(lay)
    for l in range(depth):
        lw = _layer_weights(w_in[l], mla_w_q_up[l], mla_w_kv_up[l], rwkv_w2[l], rwkv_a2[l], rwkv_g2[l],
                            w_branch[l], w_out[l], w_router_group[l], b_router_group[l],
                            w_router_expert[l], b_router_expert[l],
                            w_exp_gate[l], w_exp_up[l], w_exp_down[l])
        h = _mixer(h, norm_mix[l], lw,
                   (gdn_conv[l], gdn_a_log[l], gdn_dt_bias[l], gdn_norm[l]),
                   (mla_q_norm[l], mla_kv_norm[l]),
                   (rwkv_mu[l], rwkv_w0[l], rwkv_a0[l], rwkv_k_k[l], rwkv_k_a[l],
                    rwkv_r_k[l].reshape(-1), rwkv_ln_w[l], rwkv_ln_b[l]),
                   tables, lay)
        h = _moe(h, norm_ffn[l], lw["w_r"], lw["b_r"], lw["w_gu"], lw["w_down"], lay)
    return _final_norm(h, norm_final, lay, seq).reshape(batch, seq, D_MODEL)
```

```python
import functools
import math
from typing import NamedTuple

import jax
import jax.numpy as jnp
from jax import lax
from jax.experimental import pallas as pl
from jax.experimental.pallas import tpu as pltpu

F32 = jnp.float32
BF16 = jnp.bfloat16

D_MODEL = 2048
N_META = 16
NORM_EPS = 1e-6
L2_EPS = 1e-6

GDN_HEADS = 8
GDN_DK = 128
GDN_DV = 128
GDN_CONV = 4
GDN_QK = GDN_HEADS * GDN_DK
GDN_VW = GDN_HEADS * GDN_DV

MLA_HEADS = 8
MLA_Q_LORA = 768
MLA_KV_LORA = 512
MLA_D_NOPE = 128
MLA_D_ROPE = 64
MLA_DV = 128
MLA_DQK = 256
ROPE_THETA = 10000.0

RWKV_HEADS = 16
RWKV_N = 64
RWKV_W = RWKV_HEADS * RWKV_N
RWKV_DECAY_LORA = 64
RWKV_A_LORA = 64
RWKV_G_LORA = 160
RWKV_LORA = RWKV_DECAY_LORA + RWKV_A_LORA + RWKV_G_LORA
RWKV_LORA_PAD = 384
RWKV_LN_EPS = 64e-5
RWKV_IN = 3 * RWKV_W + RWKV_LORA
RWKV_IN_PAD = 3 * RWKV_W + RWKV_LORA_PAD

N_BRANCH = 3
BRANCH_W = 1024

N_GROUPS = 4
EXPERTS_PER_GROUP = 8
N_EXPERTS = N_GROUPS * EXPERTS_PER_GROUP
TOP_K = 2
D_EXPERT = 512

IN_SPLITS = (GDN_QK, GDN_QK, GDN_VW, GDN_VW, GDN_HEADS, GDN_HEADS,
             MLA_Q_LORA, MLA_KV_LORA, MLA_D_ROPE, RWKV_IN, N_BRANCH * D_MODEL)

LANES = 128
SUBLANES = 8
CHUNK = 64
ATTN_TILE = 256
FLASH_TILE = 768
MASK_BIAS = -1e30
MOE_TILE = 256
COMBINE_TILE = 128
VMEM_LIMIT = 48 * 1024 * 1024

NT_DIMS = (((1,), (1,)), ((), ()))
TN_DIMS = (((0,), (0,)), ((), ()))


class Layout(NamedTuple):
    batch: int
    t: int
    front: int
    tp: int

    @property
    def rows(self):
        return self.batch * self.tp


def make_layout(batch, seq):
    t = N_META + seq
    front = (-N_META) % CHUNK
    tp = -(-(front + t) // ATTN_TILE) * ATTN_TILE
    return Layout(batch, t, front, tp)


def _tile(n, pref, align):
    best = None
    for cand in range(align, min(n, pref) + 1, align):
        if n % cand == 0:
            best = cand
    if best is None:
        raise ValueError(f"no tile for {n}")
    return best


def _params(*sem):
    return pltpu.CompilerParams(dimension_semantics=sem, vmem_limit_bytes=VMEM_LIMIT)


def _valid_rows(p0, n, lay):
    p = p0 + lax.broadcasted_iota(jnp.int32, (n, 1), 0)
    p = jnp.where(p >= lay.tp, p - lay.tp, p)
    return (p >= lay.front) & (p < lay.front + lay.t)


def _split_bf16(a):
    hi = a.astype(BF16)
    lo = (a - hi.astype(F32)).astype(BF16)
    return hi, lo


def _dot(a, b, dims=None):
    a = a.astype(BF16)
    b = b.astype(BF16)
    if dims is None:
        return jnp.dot(a, b, preferred_element_type=F32)
    return lax.dot_general(a, b, dims, preferred_element_type=F32)


def _dot3(a, b):
    ah, al = _split_bf16(a)
    bh, bl = _split_bf16(b)
    return (jnp.dot(ah, bh, preferred_element_type=F32)
            + jnp.dot(al, bh, preferred_element_type=F32)
            + jnp.dot(ah, bl, preferred_element_type=F32))


INV_DOT = _dot


def _sigmoid(x):
    return 1.0 / (1.0 + jnp.exp(-x))


def _softplus(x):
    return jnp.maximum(x, 0.0) + jnp.log(1.0 + jnp.exp(-jnp.abs(x)))


def _unit_lower_inverse(a_list, row, col):
    blk16 = (row >> 4) == (col >> 4)
    blk32 = (row >> 5) == (col >> 5)
    eye = jnp.where(row == col, 1.0, 0.0)
    p = [jnp.where(blk16, -a, 0.0) for a in a_list]
    t = [eye + n for n in p]
    for _ in range(3):
        p = [INV_DOT(x, x) for x in p]
        t = [ti + INV_DOT(ti, pi) for ti, pi in zip(t, p)]
    e1 = [jnp.where(blk32 & jnp.logical_not(blk16), a, 0.0) for a in a_list]
    te = [INV_DOT(ti, ei) for ti, ei in zip(t, e1)]
    t = [ti - INV_DOT(tei, ti) for ti, tei in zip(t, te)]
    e2 = [jnp.where(blk32, 0.0, a) for a in a_list]
    te = [INV_DOT(ti, ei) for ti, ei in zip(t, e2)]
    return [ti - INV_DOT(tei, ti) for ti, tei in zip(t, te)]


def _col_to_row(col_vec, eye):
    return jnp.sum(jnp.where(eye, col_vec, 0.0), axis=0, keepdims=True)


def _matmul_kernel(*refs, tm, norm, residual, lay):
    it = iter(refs)
    a_ref = next(it)
    b_ref = next(it)
    nw_ref = next(it) if norm else None
    r_ref = next(it) if residual else None
    o_ref = next(it)
    a_scr = next(it)
    i = pl.program_id(0)

    @pl.when(pl.program_id(1) == 0)
    def _():
        a = a_ref[...].astype(F32)
        if norm:
            ms = jnp.mean(a * a, axis=-1, keepdims=True)
            a = a * lax.rsqrt(ms + NORM_EPS) * nw_ref[...]
        if lay is not None:
            a = jnp.where(_valid_rows(lax.rem(i * tm, lay.tp), tm, lay), a, 0.0)
        a_scr[...] = a.astype(BF16)

    acc = jnp.dot(a_scr[...], b_ref[...], preferred_element_type=F32)
    if residual:
        acc = acc + r_ref[...]
    o_ref[...] = acc.astype(o_ref.dtype)


def _matmul(a, b, *, norm_w=None, residual=None, lay=None, out_dtype=F32, name="matmul"):
    m, k = a.shape
    n = b.shape[1]
    tm = _tile(m, 512, SUBLANES)
    tn = _tile(n, 512, LANES)
    in_specs = [pl.BlockSpec((tm, k), lambda i, j: (i, 0)),
                pl.BlockSpec((k, tn), lambda i, j: (0, j))]
    args = [a, b]
    if norm_w is not None:
        in_specs.append(pl.BlockSpec((1, k), lambda i, j: (0, 0)))
        args.append(norm_w.reshape(1, k).astype(F32))
    if residual is not None:
        in_specs.append(pl.BlockSpec((tm, tn), lambda i, j: (i, j)))
        args.append(residual)
    return pl.pallas_call(
        functools.partial(_matmul_kernel, tm=tm, norm=norm_w is not None,
                          residual=residual is not None, lay=lay),
        out_shape=jax.ShapeDtypeStruct((m, n), out_dtype),
        grid=(m // tm, n // tn),
        in_specs=in_specs,
        out_specs=pl.BlockSpec((tm, tn), lambda i, j: (i, j)),
        scratch_shapes=[pltpu.VMEM((tm, k), BF16)],
        compiler_params=_params("arbitrary", "arbitrary"),
        name=name,
    )(*args)


def _gdn_kernel(qkv_ref, z_ref, ba_ref, cw_ref, alog_ref, dtb_ref, nw_ref, o_ref,
                xbuf, s_scr, *, lay):
    c = pl.program_id(1)
    C = CHUNK
    width = 2 * GDN_QK + GDN_VW

    @pl.when(c == 0)
    def _():
        xbuf[0:SUBLANES, :] = jnp.zeros((SUBLANES, width), F32)
        s_scr[...] = jnp.zeros_like(s_scr)

    x = qkv_ref[...]
    xbuf[SUBLANES:SUBLANES + C, :] = x
    cw = cw_ref[...]
    y = x * cw[3:4, :]
    for d in range(1, GDN_CONV):
        y = y + xbuf[SUBLANES - d:SUBLANES - d + C, :] * cw[3 - d:4 - d, :]
    xbuf[0:SUBLANES, :] = x[C - SUBLANES:C, :]
    y = y * _sigmoid(y)

    valid = _valid_rows(c * C, C, lay)
    ba = ba_ref[...]
    beta_all = _sigmoid(ba)
    g_all = -jnp.exp(alog_ref[...]) * _softplus(ba + dtb_ref[...])
    g_all = jnp.where(valid, g_all, 0.0)

    row = lax.broadcasted_iota(jnp.int32, (C, C), 0)
    col = lax.broadcasted_iota(jnp.int32, (C, C), 1)
    eye = row == col
    causal = row >= col
    strict = row > col
    gc_all = _dot3(jnp.where(causal, 1.0, 0.0), g_all)

    heads = range(GDN_HEADS)
    l2 = lambda t: t * lax.rsqrt(jnp.sum(t * t, axis=-1, keepdims=True) + L2_EPS)
    q = [l2(y[:, h * GDN_DK:(h + 1) * GDN_DK]) * GDN_DK ** -0.5 for h in heads]
    k = [l2(y[:, GDN_QK + h * GDN_DK:GDN_QK + (h + 1) * GDN_DK]) for h in heads]
    v = [y[:, 2 * GDN_QK + h * GDN_DV:2 * GDN_QK + (h + 1) * GDN_DV] for h in heads]
    beta = [beta_all[:, h:h + 1] for h in heads]
    gc = [gc_all[:, GDN_HEADS + h:GDN_HEADS + h + 1] for h in heads]
    gc_last = [t[C - 1:C, :] for t in gc]
    decay = [jnp.where(causal, jnp.exp(jnp.where(causal, t - _col_to_row(t, eye), 0.0)), 0.0)
             for t in gc]
    egc = [jnp.exp(t) for t in gc]
    kb = [k[h] * beta[h] for h in heads]
    kq = [_dot(jnp.concatenate([kb[h], q[h]], axis=0), k[h], NT_DIMS) for h in heads]
    a = [jnp.where(strict, kq[h][0:C] * decay[h], 0.0) for h in heads]
    attn = [kq[h][C:2 * C] * decay[h] for h in heads]
    t_inv = _unit_lower_inverse(a, row, col)
    uw = [_dot(t_inv[h], jnp.concatenate([v[h] * beta[h], kb[h] * egc[h]], axis=1)) for h in heads]
    s = [s_scr[h] for h in heads]
    ws = [_dot(jnp.concatenate([uw[h][:, GDN_DV:], q[h] * egc[h]], axis=0), s[h]) for h in heads]
    v_new = [uw[h][:, 0:GDN_DV] - ws[h][0:C] for h in heads]
    o = [ws[h][C:2 * C] + _dot(attn[h], v_new[h]) for h in heads]
    upd = [_dot(k[h] * jnp.exp(gc_last[h] - gc[h]), v_new[h], TN_DIMS) for h in heads]
    z = z_ref[...]
    nw = nw_ref[...]
    for h in heads:
        s_scr[h] = s[h] * jnp.exp(gc_last[h]) + upd[h]
        oh = o[h] * lax.rsqrt(jnp.mean(o[h] * o[h], axis=-1, keepdims=True) + NORM_EPS) * nw
        zh = z[:, h * GDN_DV:(h + 1) * GDN_DV]
        o_ref[:, h * GDN_DV:(h + 1) * GDN_DV] = (oh * (zh * _sigmoid(zh))).astype(o_ref.dtype)


def _gdn(pg, conv_w, a_log, dt_bias, norm_w, lay):
    C = CHUNK
    nc = lay.tp // C
    width = 2 * GDN_QK + GDN_VW
    pad = LANES - 2 * GDN_HEADS
    alog = jnp.pad(a_log.astype(F32), (GDN_HEADS, pad)).reshape(1, LANES)
    dtb = jnp.pad(dt_bias.astype(F32), (GDN_HEADS, pad)).reshape(1, LANES)
    rows = lambda b, c: b * nc + c
    return pl.pallas_call(
        functools.partial(_gdn_kernel, lay=lay),
        out_shape=jax.ShapeDtypeStruct((lay.rows, GDN_VW), BF16),
        grid=(lay.batch, nc),
        in_specs=[
            pl.BlockSpec((C, width), lambda b, c: (rows(b, c), 0)),
            pl.BlockSpec((C, GDN_VW), lambda b, c: (rows(b, c), width // GDN_VW)),
            pl.BlockSpec((C, LANES), lambda b, c: (rows(b, c), (width + GDN_VW) // LANES)),
            pl.BlockSpec((GDN_CONV, width), lambda b, c: (0, 0)),
            pl.BlockSpec((1, LANES), lambda b, c: (0, 0)),
            pl.BlockSpec((1, LANES), lambda b, c: (0, 0)),
            pl.BlockSpec((1, GDN_DV), lambda b, c: (0, 0)),
        ],
        out_specs=pl.BlockSpec((C, GDN_VW), lambda b, c: (rows(b, c), 0)),
        scratch_shapes=[pltpu.VMEM((C + SUBLANES, width), F32),
                        pltpu.VMEM((GDN_HEADS, GDN_DK, GDN_DV), F32)],
        compiler_params=_params("arbitrary", "arbitrary"),
        name="gdn",
    )(pg, pg, pg, conv_w.astype(F32), alog, dtb, norm_w.reshape(1, GDN_DV).astype(F32))


def _mla_prep_kernel(q_ref, kv_ref, kpe_ref, cos_ref, sin_ref, qo_ref, ko_ref, vo_ref, *, tm, lay):
    cos = cos_ref[...]
    sin = sin_ref[...]
    lane = lax.broadcasted_iota(jnp.int32, cos.shape, 1)
    half = MLA_D_ROPE // 2

    def rot(x):
        swapped = jnp.where(lane < half, pltpu.roll(x, LANES - half, 1), pltpu.roll(x, half, 1))
        return x * cos + swapped * sin

    bias_lane = lane == MLA_D_ROPE
    p = lax.rem(pl.program_id(0) * tm, lay.tp) + lax.broadcasted_iota(jnp.int32, (tm, 1), 0)
    k_bias = jnp.where(p < lay.front, MASK_BIAS, 0.0)
    scale = (MLA_D_NOPE + MLA_D_ROPE) ** -0.5
    k_pe = jnp.where(bias_lane, k_bias, rot(kpe_ref[...])).astype(BF16)
    for h in range(MLA_HEADS):
        lo = h * MLA_DQK
        qo_ref[:, lo:lo + MLA_D_NOPE] = (q_ref[:, lo:lo + MLA_D_NOPE] * scale).astype(BF16)
        qo_ref[:, lo + MLA_D_NOPE:lo + MLA_DQK] = jnp.where(
            bias_lane, 1.0, rot(q_ref[:, lo + MLA_D_NOPE:lo + MLA_DQK]) * scale).astype(BF16)
        ko_ref[:, lo:lo + MLA_D_NOPE] = kv_ref[:, h * MLA_D_NOPE:(h + 1) * MLA_D_NOPE].astype(BF16)
        ko_ref[:, lo + MLA_D_NOPE:lo + MLA_DQK] = k_pe
    vo_ref[...] = kv_ref[:, MLA_HEADS * MLA_D_NOPE:].astype(BF16)


def _flash_kernel(q_ref, k_ref, v_ref, o_ref, m_scr, l_scr, acc_scr, *, tile):
    qi = pl.program_id(2)
    T = tile
    q = q_ref[...]
    m_scr[...] = jnp.full_like(m_scr, -jnp.inf)
    l_scr[...] = jnp.zeros_like(l_scr)
    acc_scr[...] = jnp.zeros_like(acc_scr)

    def step(ki, diagonal):
        start = pl.multiple_of(ki * T, T)
        k = k_ref[pl.ds(start, T), :]
        v = v_ref[pl.ds(start, T), :]
        s = lax.dot_general(q, k, NT_DIMS, preferred_element_type=F32)
        if diagonal:
            causal = (lax.broadcasted_iota(jnp.int32, (T, T), 1)
                      <= lax.broadcasted_iota(jnp.int32, (T, T), 0))
            s = jnp.where(causal, s, -jnp.inf)
        m_prev = m_scr[...]
        m_new = jnp.maximum(m_prev, jnp.max(s, axis=-1, keepdims=True))
        p = jnp.exp(s - m_new)
        alpha = jnp.exp(m_prev - m_new)
        l_scr[...] = alpha * l_scr[...] + jnp.sum(p, axis=-1, keepdims=True)
        acc_scr[...] = alpha * acc_scr[...] + jnp.dot(p.astype(BF16), v, preferred_element_type=F32)
        m_scr[...] = m_new

    def body(ki, carry):
        step(ki, False)
        return carry

    lax.fori_loop(0, qi, body, 0)
    step(qi, True)
    o_ref[...] = (acc_scr[...] / l_scr[...]).astype(o_ref.dtype)


def _rope_tables(lay):
    pos = (jnp.arange(lay.tp) - lay.front).astype(F32)
    inv_freq = ROPE_THETA ** (-jnp.arange(0, MLA_D_ROPE, 2, dtype=F32) / MLA_D_ROPE)
    ang = pos[:, None] * inv_freq[None, :]
    cos, sin = jnp.cos(ang), jnp.sin(ang)
    zeros = jnp.zeros((lay.tp, LANES - MLA_D_ROPE), F32)
    return (jnp.concatenate([cos, cos, zeros], axis=1),
            jnp.concatenate([-sin, sin, zeros], axis=1))


def _mla(p_cq, p_ckv, p_kpe, q_norm, wq, kv_norm, wkv, tables, lay):
    qraw = _matmul(p_cq, wq, norm_w=q_norm, name="mla_q_up")
    kvraw = _matmul(p_ckv, wkv, norm_w=kv_norm, name="mla_kv_up")
    rows = lay.rows
    tm = ATTN_TILE
    nt = lay.tp // tm
    hq = MLA_HEADS * MLA_DQK
    hv = MLA_HEADS * MLA_DV
    cos, sin = tables
    q, k, v = pl.pallas_call(
        functools.partial(_mla_prep_kernel, tm=tm, lay=lay),
        out_shape=(jax.ShapeDtypeStruct((rows, hq), BF16),
                   jax.ShapeDtypeStruct((rows, hq), BF16),
                   jax.ShapeDtypeStruct((rows, hv), BF16)),
        grid=(rows // tm,),
        in_specs=[pl.BlockSpec((tm, hq), lambda i: (i, 0)),
                  pl.BlockSpec((tm, MLA_HEADS * (MLA_D_NOPE + MLA_DV)), lambda i: (i, 0)),
                  pl.BlockSpec((tm, LANES), lambda i: (i, 0)),
                  pl.BlockSpec((tm, LANES), lambda i: (i % nt, 0)),
                  pl.BlockSpec((tm, LANES), lambda i: (i % nt, 0))],
        out_specs=(pl.BlockSpec((tm, hq), lambda i: (i, 0)),
                   pl.BlockSpec((tm, hq), lambda i: (i, 0)),
                   pl.BlockSpec((tm, hv), lambda i: (i, 0))),
        compiler_params=_params("arbitrary"),
        name="mla_prep",
    )(qraw, kvraw, p_kpe, cos, sin)
    ta = _tile(lay.tp, FLASH_TILE, ATTN_TILE)
    na = lay.tp // ta
    return pl.pallas_call(
        functools.partial(_flash_kernel, tile=ta),
        out_shape=jax.ShapeDtypeStruct((rows, hv), BF16),
        grid=(lay.batch, MLA_HEADS, na),
        in_specs=[pl.BlockSpec((ta, MLA_DQK), lambda b, h, i: (b * na + i, h)),
                  pl.BlockSpec((lay.tp, MLA_DQK), lambda b, h, i: (b, h)),
                  pl.BlockSpec((lay.tp, MLA_DV), lambda b, h, i: (b, h))],
        out_specs=pl.BlockSpec((ta, MLA_DV), lambda b, h, i: (b * na + i, h)),
        scratch_shapes=[pltpu.VMEM((ta, 1), F32), pltpu.VMEM((ta, 1), F32),
                        pltpu.VMEM((ta, MLA_DV), F32)],
        compiler_params=_params("arbitrary", "arbitrary", "arbitrary"),
        name="mla_attention",
    )(q, k, v)


def _rwkv_kernel(x_ref, mu_ref, wl_ref, w0_ref, a0_ref, kk_ref, ka_ref, rk_ref, lnw_ref, lnb_ref,
                 o_ref, xbuf, s_scr, *, lay):
    c = pl.program_id(1)
    C = CHUNK
    W = RWKV_W
    N = RWKV_N

    @pl.when(c == 0)
    def _():
        xbuf[0:SUBLANES, :] = jnp.zeros((SUBLANES, RWKV_IN_PAD), F32)
        s_scr[...] = jnp.zeros_like(s_scr)

    x = x_ref[...]
    xbuf[SUBLANES:SUBLANES + C, :] = x
    prev = xbuf[SUBLANES - 1:SUBLANES - 1 + C, :]
    xbuf[0:SUBLANES, :] = x[C - SUBLANES:C, :]
    xm = x + (prev - x) * mu_ref[...]
    xm = jnp.where(_valid_rows(c * C, C, lay), xm, 0.0)
    r = xm[:, 0:W]
    k = xm[:, W:2 * W]
    v = xm[:, 2 * W:3 * W]
    lin = xm[:, 3 * W:]
    lane = lax.broadcasted_iota(jnp.int32, lin.shape, 1)
    d1 = RWKV_DECAY_LORA
    d2 = d1 + RWKV_A_LORA
    lin = jnp.where(lane < d1, jnp.tanh(lin),
                    jnp.where(lane < d2, lin, jnp.where(lane < RWKV_LORA, _sigmoid(lin), 0.0)))
    lo = _dot(lin, wl_ref[...])
    w_log = -_softplus(-(w0_ref[...] + lo[:, 0:W])) - 0.5
    lw = -jnp.exp(w_log)
    a = _sigmoid(a0_ref[...] + lo[:, W:2 * W])
    g = lo[:, 2 * W:3 * W]
    kk_raw = k * kk_ref[...]
    k2 = k * (1.0 + (a - 1.0) * ka_ref[...])
    rkr = r * k2 * rk_ref[...]

    row = lax.broadcasted_iota(jnp.int32, (C, C), 0)
    col = lax.broadcasted_iota(jnp.int32, (C, C), 1)
    causal = row >= col
    strict = row > col
    cum = _dot3(jnp.where(causal, 1.0, 0.0), lw)
    c_last = cum[C - 1:C, :]
    e_prev = jnp.exp(cum - lw)
    e_neg = jnp.exp(-cum)
    e_pos = jnp.exp(cum)
    e_rest = jnp.exp(c_last - cum)
    e_last = jnp.exp(c_last)
    lnw = lnw_ref[...]
    lnb = lnb_ref[...]

    heads = range(RWKV_HEADS)
    sl = [slice(h * N, (h + 1) * N) for h in heads]
    kk = [kk_raw[:, x] for x in sl]
    kk = [t * lax.rsqrt(jnp.sum(t * t, axis=-1, keepdims=True) + L2_EPS) for t in kk]
    vh = [v[:, x] for x in sl]
    b_vec = [kk[h] * a[:, sl[h]] for h in heads]
    ar = [jnp.concatenate([-kk[h] * e_prev[:, sl[h]], r[:, sl[h]] * e_pos[:, sl[h]]], axis=0)
          for h in heads]
    bk = [jnp.concatenate([b_vec[h] * e_neg[:, sl[h]], k2[:, sl[h]] * e_neg[:, sl[h]]], axis=0)
          for h in heads]
    big = [_dot(ar[h], bk[h], NT_DIMS) for h in heads]
    a_ab = [jnp.where(strict, t[0:C, 0:C], 0.0) for t in big]
    ak_rk = [jnp.concatenate([jnp.where(strict, t[0:C, C:2 * C], 0.0),
                              jnp.where(causal, t[C:2 * C, C:2 * C], 0.0)], axis=0) for t in big]
    r_b = [jnp.where(causal, t[C:2 * C, 0:C], 0.0) for t in big]
    t_inv = _unit_lower_inverse([-t for t in a_ab], row, col)
    s = [s_scr[h] for h in heads]
    ar_s = [_dot(ar[h], s[h], NT_DIMS) for h in heads]
    akv = [_dot(ak_rk[h], vh[h]) for h in heads]
    u = [_dot(t_inv[h], ar_s[h][0:C] + akv[h][0:C]) for h in heads]
    yh = [ar_s[h][C:2 * C] + akv[h][C:2 * C] + _dot(r_b[h], u[h]) for h in heads]
    upd = [_dot(jnp.concatenate([u[h], vh[h]], axis=0),
                jnp.concatenate([b_vec[h] * e_rest[:, sl[h]], k2[:, sl[h]] * e_rest[:, sl[h]]], axis=0),
                TN_DIMS) for h in heads]
    for h in heads:
        s_scr[h] = s[h] * e_last[:, sl[h]] + upd[h]
        mean = jnp.mean(yh[h], axis=-1, keepdims=True)
        cen = yh[h] - mean
        var = jnp.mean(cen * cen, axis=-1, keepdims=True)
        yn = cen * lax.rsqrt(var + RWKV_LN_EPS) * lnw[:, sl[h]] + lnb[:, sl[h]]
        bonus = jnp.sum(rkr[:, sl[h]], axis=-1, keepdims=True) * vh[h]
        o_ref[:, sl[h]] = ((yn + bonus) * g[:, sl[h]]).astype(o_ref.dtype)


def _rwkv(p_rw, mu, w_lora, w0, a0, k_k, k_a, r_k, ln_w, ln_b, lay):
    C = CHUNK
    nc = lay.tp // C
    vec = lambda t: t.reshape(1, RWKV_W).astype(F32)
    rows = lambda b, c: (b * nc + c, 0)
    const = lambda b, c: (0, 0)
    return pl.pallas_call(
        functools.partial(_rwkv_kernel, lay=lay),
        out_shape=jax.ShapeDtypeStruct((lay.rows, RWKV_W), BF16),
        grid=(lay.batch, nc),
        in_specs=[pl.BlockSpec((C, RWKV_IN_PAD), rows),
                  pl.BlockSpec((1, RWKV_IN_PAD), const),
                  pl.BlockSpec((RWKV_LORA_PAD, 3 * RWKV_W), const)]
                 + [pl.BlockSpec((1, RWKV_W), const)] * 7,
        out_specs=pl.BlockSpec((C, RWKV_W), rows),
        scratch_shapes=[pltpu.VMEM((C + SUBLANES, RWKV_IN_PAD), F32),
                        pltpu.VMEM((RWKV_HEADS, RWKV_N, RWKV_N), F32)],
        compiler_params=_params("arbitrary", "arbitrary"),
        name="rwkv7",
    )(p_rw, jnp.pad(mu.astype(F32), (0, RWKV_IN_PAD - RWKV_IN)).reshape(1, RWKV_IN_PAD), w_lora,
      vec(w0), vec(a0), vec(k_k), vec(k_a), vec(r_k), vec(ln_w), vec(ln_b))


def _rwkv_lora_weight(w2, a2, g2):
    w = jnp.zeros((RWKV_LORA_PAD, 3 * RWKV_W), F32)
    d1 = RWKV_DECAY_LORA
    d2 = d1 + RWKV_A_LORA
    w = w.at[0:d1, 0:RWKV_W].set(w2)
    w = w.at[d1:d2, RWKV_W:2 * RWKV_W].set(a2)
    w = w.at[d2:RWKV_LORA, 2 * RWKV_W:].set(g2)
    return w.astype(BF16)


def _merge_kernel(ya_ref, yb_ref, yc_ref, wb_ref, g0_ref, g1_ref, g2_ref, o_ref):
    acc = _sigmoid(g0_ref[...]) * jnp.dot(ya_ref[...], wb_ref[0], preferred_element_type=F32)
    acc += _sigmoid(g1_ref[...]) * jnp.dot(yb_ref[...], wb_ref[1], preferred_element_type=F32)
    acc += _sigmoid(g2_ref[...]) * jnp.dot(yc_ref[...], wb_ref[2], preferred_element_type=F32)
    o_ref[...] = acc.astype(o_ref.dtype)


def _merge(ya, yb, yc, w_branch, gates):
    rows = ya.shape[0]
    tm = _tile(rows, 512, SUBLANES)
    tn = 512
    nj = D_MODEL // tn
    y_spec = pl.BlockSpec((tm, BRANCH_W), lambda i, j: (i, 0))
    gate = lambda br: pl.BlockSpec((tm, tn), lambda i, j: (i, br * nj + j))
    return pl.pallas_call(
        _merge_kernel,
        out_shape=jax.ShapeDtypeStruct((rows, D_MODEL), BF16),
        grid=(rows // tm, nj),
        in_specs=[y_spec, y_spec, y_spec,
                  pl.BlockSpec((N_BRANCH, BRANCH_W, tn), lambda i, j: (0, 0, j)),
                  gate(0), gate(1), gate(2)],
        out_specs=pl.BlockSpec((tm, tn), lambda i, j: (i, j)),
        compiler_params=_params("arbitrary", "arbitrary"),
        name="gated_merge",
    )(ya, yb, yc, w_branch, gates, gates, gates)


def _router_kernel(h_ref, nw_ref, wr_ref, br_ref, xt_ref, idx_ref, gw_ref, *, tm, lay):
    i = pl.program_id(0)
    x = h_ref[...]
    x = x * lax.rsqrt(jnp.mean(x * x, axis=-1, keepdims=True) + NORM_EPS) * nw_ref[...]
    x = jnp.where(_valid_rows(lax.rem(i * tm, lay.tp), tm, lay), x, 0.0)
    xt_ref[...] = x
    logits = _dot3(x, wr_ref[...]) + br_ref[...]
    lane = lax.broadcasted_iota(jnp.int32, logits.shape, 1)
    neg = -jnp.inf

    def top1(vals):
        m = jnp.max(vals, axis=-1, keepdims=True)
        idx = jnp.min(jnp.where(vals == m, lane, LANES), axis=-1, keepdims=True)
        return m, idx

    g_mask = lane < N_GROUPS
    g_top, g_idx = top1(jnp.where(g_mask, logits, neg))
    p_group = 1.0 / jnp.sum(jnp.where(g_mask, jnp.exp(logits - g_top), 0.0), axis=-1, keepdims=True)
    e_lo = N_GROUPS + g_idx * EXPERTS_PER_GROUP
    e_vals = jnp.where((lane >= e_lo) & (lane < e_lo + EXPERTS_PER_GROUP), logits, neg)
    e1, i1 = top1(e_vals)
    e2, i2 = top1(jnp.where(lane == i1, neg, e_vals))
    t = jnp.exp(e2 - e1)
    w1 = p_group / (1.0 + t)
    w2 = p_group * t / (1.0 + t)
    idx_ref[...] = jnp.where(lane == 0, i1 - N_GROUPS, jnp.where(lane == 1, i2 - N_GROUPS, 0))
    gw_ref[...] = jnp.where(lane == 0, w1, jnp.where(lane == 1, w2, 0.0))


def _router(h, norm_w, w_r, b_r, lay):
    rows = h.shape[0]
    tm = _tile(rows, 512, SUBLANES)
    row_spec = lambda w: pl.BlockSpec((tm, w), lambda i: (i, 0))
    return pl.pallas_call(
        functools.partial(_router_kernel, tm=tm, lay=lay),
        out_shape=(jax.ShapeDtypeStruct((rows, D_MODEL), F32),
                   jax.ShapeDtypeStruct((rows, LANES), jnp.int32),
                   jax.ShapeDtypeStruct((rows, LANES), F32)),
        grid=(rows // tm,),
        in_specs=[row_spec(D_MODEL),
                  pl.BlockSpec((1, D_MODEL), lambda i: (0, 0)),
                  pl.BlockSpec((D_MODEL, LANES), lambda i: (0, 0)),
                  pl.BlockSpec((1, LANES), lambda i: (0, 0))],
        out_specs=(row_spec(D_MODEL), row_spec(LANES), row_spec(LANES)),
        compiler_params=_params("arbitrary"),
        name="moe_router",
    )(h, norm_w.reshape(1, D_MODEL).astype(F32), w_r, b_r)


def _row_gather(idx_ref, base, n, src_hbm, dst, sem):
    def body(r, carry):
        tok = idx_ref[base + r]
        pltpu.make_async_copy(src_hbm.at[pl.ds(tok, 1), :], dst.at[pl.ds(r, 1), :], sem).start()
        return carry
    lax.fori_loop(0, n, body, 0)


def _row_gather_wait(n, src_hbm, dst, sem):
    def body(r, carry):
        pltpu.make_async_copy(src_hbm.at[pl.ds(0, 1), :], dst.at[pl.ds(r, 1), :], sem).wait()
        return carry
    lax.fori_loop(0, n, body, 0)


def _expert_kernel(be_ref, tok_ref, nused_ref, xt_hbm, wgu_ref, wd_ref, o_ref, xbuf, sem):
    i = pl.program_id(0)
    nb = pl.num_programs(0)
    tb = MOE_TILE
    slot = i % 2
    used = nused_ref[0]

    @pl.when((i == 0) & (used > 0))
    def _():
        _row_gather(tok_ref, 0, tb, xt_hbm, xbuf.at[0], sem.at[0])

    @pl.when((i + 1 < nb) & (i + 1 < used))
    def _():
        _row_gather(tok_ref, (i + 1) * tb, tb, xt_hbm, xbuf.at[1 - slot], sem.at[1 - slot])

    @pl.when(i < used)
    def _():
        _row_gather_wait(tb, xt_hbm, xbuf.at[slot], sem.at[slot])
        x = xbuf[slot].astype(BF16)
        gu = jnp.dot(x, wgu_ref[0], preferred_element_type=F32)
        gate = gu[:, :D_EXPERT]
        act = (gate * _sigmoid(gate) * gu[:, D_EXPERT:]).astype(BF16)
        o_ref[...] = jnp.dot(act, wd_ref[0], preferred_element_type=F32)

    @pl.when(i >= used)
    def _():
        o_ref[...] = jnp.zeros_like(o_ref)


def _experts(xt, block_expert, row_token, n_used, w_gu, w_down):
    n_rows = row_token.shape[0]
    nb = n_rows // MOE_TILE
    grid_spec = pltpu.PrefetchScalarGridSpec(
        num_scalar_prefetch=3,
        grid=(nb,),
        in_specs=[pl.BlockSpec(memory_space=pl.ANY),
                  pl.BlockSpec((1, D_MODEL, 2 * D_EXPERT), lambda i, be, tok, nu: (be[i], 0, 0)),
                  pl.BlockSpec((1, D_EXPERT, D_MODEL), lambda i, be, tok, nu: (be[i], 0, 0))],
        out_specs=pl.BlockSpec((MOE_TILE, D_MODEL), lambda i, be, tok, nu: (i, 0)),
        scratch_shapes=[pltpu.VMEM((2, MOE_TILE, D_MODEL), F32),
                        pltpu.SemaphoreType.DMA((2,))],
    )
    return pl.pallas_call(
        _expert_kernel,
        out_shape=jax.ShapeDtypeStruct((n_rows, D_MODEL), F32),
        grid_spec=grid_spec,
        compiler_params=_params("arbitrary"),
        name="moe_experts",
    )(block_expert, row_token, n_used, xt, w_gu, w_down)


def _combine_kernel(dest0_ref, dest1_ref, y_hbm, h_ref, gw_ref, o_ref, buf0, buf1, sem):
    i = pl.program_id(0)
    nb = pl.num_programs(0)
    tc = COMBINE_TILE
    slot = i % 2

    def start(step, s):
        _row_gather(dest0_ref, step * tc, tc, y_hbm, buf0.at[s], sem.at[0, s])
        _row_gather(dest1_ref, step * tc, tc, y_hbm, buf1.at[s], sem.at[1, s])

    @pl.when(i == 0)
    def _():
        start(0, 0)

    @pl.when(i + 1 < nb)
    def _():
        start(i + 1, 1 - slot)

    _row_gather_wait(tc, y_hbm, buf0.at[slot], sem.at[0, slot])
    _row_gather_wait(tc, y_hbm, buf1.at[slot], sem.at[1, slot])
    gw = gw_ref[...]
    o_ref[...] = h_ref[...] + gw[:, 0:1] * buf0[slot] + gw[:, 1:2] * buf1[slot]


def _combine(h, y_rows, dest0, dest1, gw):
    rows = h.shape[0]
    tc = COMBINE_TILE
    grid_spec = pltpu.PrefetchScalarGridSpec(
        num_scalar_prefetch=2,
        grid=(rows // tc,),
        in_specs=[pl.BlockSpec(memory_space=pl.ANY),
                  pl.BlockSpec((tc, D_MODEL), lambda i, d0, d1: (i, 0)),
                  pl.BlockSpec((tc, LANES), lambda i, d0, d1: (i, 0))],
        out_specs=pl.BlockSpec((tc, D_MODEL), lambda i, d0, d1: (i, 0)),
        scratch_shapes=[pltpu.VMEM((2, tc, D_MODEL), F32),
                        pltpu.VMEM((2, tc, D_MODEL), F32),
                        pltpu.SemaphoreType.DMA((2, 2))],
    )
    return pl.pallas_call(
        _combine_kernel,
        out_shape=jax.ShapeDtypeStruct((rows, D_MODEL), F32),
        grid_spec=grid_spec,
        compiler_params=_params("arbitrary"),
        name="moe_combine",
    )(dest0, dest1, y_rows, h, gw)


def _moe(h, norm_w, w_r, b_r, w_gu, w_down, lay):
    rows = h.shape[0]
    xt, idx, gw = _router(h, norm_w, w_r, b_r, lay)
    expert = idx[:, :TOP_K].reshape(-1)
    n_assign = rows * TOP_K
    onehot = (expert[:, None] == jnp.arange(N_EXPERTS, dtype=jnp.int32)[None, :]).astype(jnp.int32)
    csum = jnp.cumsum(onehot, axis=0)
    counts = csum[-1]
    rank = jnp.sum(csum * onehot, axis=1) - 1
    padded = (counts + MOE_TILE - 1) // MOE_TILE * MOE_TILE
    padded_end = jnp.cumsum(padded)
    padded_start = padded_end - padded
    dest = (padded_start[expert] + rank).astype(jnp.int32)
    n_rows = (-(-n_assign // MOE_TILE) + N_EXPERTS) * MOE_TILE
    nb = n_rows // MOE_TILE
    row_token = jnp.zeros((n_rows,), jnp.int32).at[dest].set(
        jnp.arange(n_assign, dtype=jnp.int32) // TOP_K, unique_indices=True)
    block_expert = jnp.minimum(
        jnp.searchsorted(padded_end, jnp.arange(nb, dtype=jnp.int32) * MOE_TILE, side="right"),
        N_EXPERTS - 1).astype(jnp.int32)
    n_used = (padded_end[-1:] // MOE_TILE).astype(jnp.int32)
    y_rows = _experts(xt, block_expert, row_token, n_used, w_gu, w_down)
    dest2 = dest.reshape(rows, TOP_K)
    return _combine(h, y_rows, dest2[:, 0], dest2[:, 1], gw)


def _final_norm_kernel(h_ref, w_ref, o_ref):
    x = h_ref[...]
    o_ref[...] = x * lax.rsqrt(jnp.mean(x * x, axis=-1, keepdims=True) + NORM_EPS) * w_ref[...]


def _final_norm(h, w, lay, seq):
    first = lay.front + N_META
    tm = math.gcd(math.gcd(first, seq), lay.tp)
    per_b = lay.tp // tm
    nseq = seq // tm
    return pl.pallas_call(
        _final_norm_kernel,
        out_shape=jax.ShapeDtypeStruct((lay.batch * seq, D_MODEL), F32),
        grid=(lay.batch, nseq),
        in_specs=[pl.BlockSpec((tm, D_MODEL), lambda b, i: (b * per_b + first // tm + i, 0)),
                  pl.BlockSpec((1, D_MODEL), lambda b, i: (0, 0))],
        out_specs=pl.BlockSpec((tm, D_MODEL), lambda b, i: (b * nseq + i, 0)),
        compiler_params=_params("arbitrary", "arbitrary"),
        name="final_norm",
    )(h, w.reshape(1, D_MODEL).astype(F32))


def _pad_cols(w, width):
    return jnp.pad(w, ((0, 0), (0, width - w.shape[1])))


def _layer_weights(w_in, mla_w_q_up, mla_w_kv_up, w2, a2, g2, w_branch, w_out,
                   w_rg, b_rg, w_re, b_re, w_gate, w_up, w_down):
    offs = [0]
    for s in IN_SPLITS:
        offs.append(offs[-1] + s)
    seg = lambda i, j: w_in[:, offs[i]:offs[j]]
    w_gdn = jnp.concatenate([seg(0, 4), _pad_cols(seg(4, 6), LANES)], axis=1).astype(BF16)
    w_cq = seg(6, 7).astype(BF16)
    w_ckv = seg(7, 8).astype(BF16)
    w_kpe = _pad_cols(seg(8, 9), LANES).astype(BF16)
    w_rw = _pad_cols(seg(9, 10), RWKV_IN_PAD).astype(BF16)
    w_gates = seg(10, 11).astype(BF16)
    wq = mla_w_q_up.reshape(MLA_Q_LORA, MLA_HEADS, MLA_D_NOPE + MLA_D_ROPE)
    wq = jnp.pad(wq, ((0, 0), (0, 0), (0, MLA_DQK - MLA_D_NOPE - MLA_D_ROPE)))
    wq = wq.reshape(MLA_Q_LORA, MLA_HEADS * MLA_DQK).astype(BF16)
    wkv = mla_w_kv_up.reshape(MLA_KV_LORA, MLA_HEADS, MLA_D_NOPE + MLA_DV)
    wkv = jnp.concatenate([wkv[:, :, :MLA_D_NOPE].reshape(MLA_KV_LORA, -1),
                           wkv[:, :, MLA_D_NOPE:].reshape(MLA_KV_LORA, -1)], axis=1).astype(BF16)
    w_r = _pad_cols(jnp.concatenate([w_rg, w_re], axis=1), LANES).astype(F32)
    b_r = jnp.pad(jnp.concatenate([b_rg, b_re]), (0, LANES - N_GROUPS - N_EXPERTS)).reshape(1, LANES)
    return dict(
        w_gdn=w_gdn, w_cq=w_cq, w_ckv=w_ckv, w_kpe=w_kpe, w_rw=w_rw, w_gates=w_gates,
        wq=wq, wkv=wkv, w_lora=_rwkv_lora_weight(w2, a2, g2),
        w_branch=w_branch.astype(BF16), w_out=w_out.astype(BF16), w_r=w_r, b_r=b_r.astype(F32),
        w_gu=jnp.concatenate([w_gate, w_up], axis=2).astype(BF16), w_down=w_down.astype(BF16))


def _mixer(h, norm_w, lw, gdn_p, mla_p, rwkv_p, tables, lay):
    proj = lambda w, name: _matmul(h, w, norm_w=norm_w, lay=lay, name=name)
    ya = _gdn(proj(lw["w_gdn"], "proj_gdn"), *gdn_p, lay)
    yb = _mla(proj(lw["w_cq"], "proj_cq"), proj(lw["w_ckv"], "proj_ckv"), proj(lw["w_kpe"], "proj_kpe"),
              mla_p[0], lw["wq"], mla_p[1], lw["wkv"], tables, lay)
    yc = _rwkv(proj(lw["w_rw"], "proj_rwkv"), rwkv_p[0], lw["w_lora"], *rwkv_p[1:], lay)
    merged = _merge(ya, yb, yc, lw["w_branch"], proj(lw["w_gates"], "proj_gates"))
    return _matmul(merged, lw["w_out"], residual=h, name="mixer_out")


def kernel(x, meta_tokens, norm_mix, w_in, gdn_conv, gdn_a_log, gdn_dt_bias, gdn_norm, mla_q_norm, mla_w_q_up, mla_kv_norm, mla_w_kv_up, rwkv_mu, rwkv_w0, rwkv_w2, rwkv_a0, rwkv_a2, rwkv_g2, rwkv_k_k, rwkv_k_a, rwkv_r_k, rwkv_ln_w, rwkv_ln_b, w_branch, w_out, norm_ffn, w_router_group, b_router_group, w_router_expert, b_router_expert, w_exp_gate, w_exp_up, w_exp_down, norm_final):
    batch, seq, _ = x.shape
    depth = w_in.shape[0]
    lay = make_layout(batch, seq)
    meta = jnp.broadcast_to(meta_tokens[None].astype(x.dtype), (batch, N_META, D_MODEL))
    h = jnp.concatenate([jnp.zeros((batch, lay.front, D_MODEL), x.dtype), meta, x,
                         jnp.zeros((batch, lay.tp - lay.front - lay.t, D_MODEL), x.dtype)], axis=1)
    h = h.reshape(lay.rows, D_MODEL)
    tables = _rope_tables(lay)
    for l in range(depth):
        lw = _layer_weights(w_in[l], mla_w_q_up[l], mla_w_kv_up[l], rwkv_w2[l], rwkv_a2[l], rwkv_g2[l],
                            w_branch[l], w_out[l], w_router_group[l], b_router_group[l],
                            w_router_expert[l], b_router_expert[l],
                            w_exp_gate[l], w_exp_up[l], w_exp_down[l])
        h = _mixer(h, norm_mix[l], lw,
                   (gdn_conv[l], gdn_a_log[l], gdn_dt_bias[l], gdn_norm[l]),
                   (mla_q_norm[l], mla_kv_norm[l]),
                   (rwkv_mu[l], rwkv_w0[l], rwkv_a0[l], rwkv_k_k[l], rwkv_k_a[l],
                    rwkv_r_k[l].reshape(-1), rwkv_ln_w[l], rwkv_ln_b[l]),
                   tables, lay)
        h = _moe(h, norm_ffn[l], lw["w_r"], lw["b_r"], lw["w_gu"], lw["w_down"], lay)
    return _final_norm(h, norm_final, lay, seq).reshape(batch, seq, D_MODEL)
```

```python
import functools
import math
from typing import NamedTuple

import jax
import jax.numpy as jnp
from jax import lax
from jax.experimental import pallas as pl
from jax.experimental.pallas import tpu as pltpu

F32 = jnp.float32
BF16 = jnp.bfloat16

D_MODEL = 2048
N_META = 16
NORM_EPS = 1e-6
L2_EPS = 1e-6

GDN_HEADS = 8
GDN_DK = 128
GDN_DV = 128
GDN_CONV = 4
GDN_QK = GDN_HEADS * GDN_DK
GDN_VW = GDN_HEADS * GDN_DV

MLA_HEADS = 8
MLA_Q_LORA = 768
MLA_KV_LORA = 512
MLA_D_NOPE = 128
MLA_D_ROPE = 64
MLA_DV = 128
MLA_DQK = 256
ROPE_THETA = 10000.0

RWKV_HEADS = 16
RWKV_N = 64
RWKV_W = RWKV_HEADS * RWKV_N
RWKV_DECAY_LORA = 64
RWKV_A_LORA = 64
RWKV_G_LORA = 160
RWKV_LORA = RWKV_DECAY_LORA + RWKV_A_LORA + RWKV_G_LORA
RWKV_LORA_PAD = 384
RWKV_LN_EPS = 64e-5
RWKV_IN = 3 * RWKV_W + RWKV_LORA
RWKV_IN_PAD = 3 * RWKV_W + RWKV_LORA_PAD

N_BRANCH = 3
BRANCH_W = 1024

N_GROUPS = 4
EXPERTS_PER_GROUP = 8
N_EXPERTS = N_GROUPS * EXPERTS_PER_GROUP
TOP_K = 2
D_EXPERT = 512

IN_SPLITS = (GDN_QK, GDN_QK, GDN_VW, GDN_VW, GDN_HEADS, GDN_HEADS,
             MLA_Q_LORA, MLA_KV_LORA, MLA_D_ROPE, RWKV_IN, N_BRANCH * D_MODEL)

LANES = 128
SUBLANES = 8
CHUNK = 64
ATTN_TILE = 256
FLASH_TILE = 768
MASK_BIAS = -1e30
MOE_TILE = 256
COMBINE_TILE = 128
GATHER_UNROLL = 8
VMEM_LIMIT = 48 * 1024 * 1024

NT_DIMS = (((1,), (1,)), ((), ()))
TN_DIMS = (((0,), (0,)), ((), ()))


class Layout(NamedTuple):
    batch: int
    t: int
    front: int
    tp: int

    @property
    def rows(self):
        return self.batch * self.tp


def make_layout(batch, seq):
    t = N_META + seq
    front = (-N_META) % CHUNK
    tp = -(-(front + t) // ATTN_TILE) * ATTN_TILE
    return Layout(batch, t, front, tp)


def _tile(n, pref, align):
    best = None
    for cand in range(align, min(n, pref) + 1, align):
        if n % cand == 0:
            best = cand
    if best is None:
        raise ValueError(f"no tile for {n}")
    return best


def _params(*sem):
    return pltpu.CompilerParams(dimension_semantics=sem, vmem_limit_bytes=VMEM_LIMIT)


def _valid_rows(p0, n, lay):
    p = p0 + lax.broadcasted_iota(jnp.int32, (n, 1), 0)
    p = jnp.where(p >= lay.tp, p - lay.tp, p)
    return (p >= lay.front) & (p < lay.front + lay.t)


def _split_bf16(a):
    hi = a.astype(BF16)
    lo = (a - hi.astype(F32)).astype(BF16)
    return hi, lo


def _dot(a, b, dims=None):
    a = a.astype(BF16)
    b = b.astype(BF16)
    if dims is None:
        return jnp.dot(a, b, preferred_element_type=F32)
    return lax.dot_general(a, b, dims, preferred_element_type=F32)


def _dot3(a, b):
    ah, al = _split_bf16(a)
    bh, bl = _split_bf16(b)
    return (jnp.dot(ah, bh, preferred_element_type=F32)
            + jnp.dot(al, bh, preferred_element_type=F32)
            + jnp.dot(ah, bl, preferred_element_type=F32))


INV_DOT = _dot


def _sigmoid(x):
    return 1.0 / (1.0 + jnp.exp(-x))


def _softplus(x):
    return jnp.maximum(x, 0.0) + jnp.log(1.0 + jnp.exp(-jnp.abs(x)))


def _unit_lower_inverse(a_list, row, col):
    blk16 = (row >> 4) == (col >> 4)
    blk32 = (row >> 5) == (col >> 5)
    eye = jnp.where(row == col, 1.0, 0.0)
    p = [jnp.where(blk16, -a, 0.0) for a in a_list]
    t = [eye + n for n in p]
    for _ in range(3):
        p = [INV_DOT(x, x) for x in p]
        t = [ti + INV_DOT(ti, pi) for ti, pi in zip(t, p)]
    e1 = [jnp.where(blk32 & jnp.logical_not(blk16), a, 0.0) for a in a_list]
    te = [INV_DOT(ti, ei) for ti, ei in zip(t, e1)]
    t = [ti - INV_DOT(tei, ti) for ti, tei in zip(t, te)]
    e2 = [jnp.where(blk32, 0.0, a) for a in a_list]
    te = [INV_DOT(ti, ei) for ti, ei in zip(t, e2)]
    return [ti - INV_DOT(tei, ti) for ti, tei in zip(t, te)]


def _col_to_row(col_vec, eye):
    return jnp.sum(jnp.where(eye, col_vec, 0.0), axis=0, keepdims=True)


def _matmul_kernel(*refs, tm, norm, residual, lay):
    it = iter(refs)
    a_ref = next(it)
    b_ref = next(it)
    nw_ref = next(it) if norm else None
    r_ref = next(it) if residual else None
    o_ref = next(it)
    a_scr = next(it)
    i = pl.program_id(0)

    @pl.when(pl.program_id(1) == 0)
    def _():
        a = a_ref[...].astype(F32)
        if norm:
            ms = jnp.mean(a * a, axis=-1, keepdims=True)
            a = a * lax.rsqrt(ms + NORM_EPS) * nw_ref[...]
        if lay is not None:
            a = jnp.where(_valid_rows(lax.rem(i * tm, lay.tp), tm, lay), a, 0.0)
        a_scr[...] = a.astype(BF16)

    acc = jnp.dot(a_scr[...], b_ref[...], preferred_element_type=F32)
    if residual:
        acc = acc + r_ref[...]
    o_ref[...] = acc.astype(o_ref.dtype)


def _matmul(a, b, *, norm_w=None, residual=None, lay=None, out_dtype=F32, name="matmul"):
    m, k = a.shape
    n = b.shape[1]
    tm = _tile(m, 512, SUBLANES)
    tn = _tile(n, 512, LANES)
    in_specs = [pl.BlockSpec((tm, k), lambda i, j: (i, 0)),
                pl.BlockSpec((k, tn), lambda i, j: (0, j))]
    args = [a, b]
    if norm_w is not None:
        in_specs.append(pl.BlockSpec((1, k), lambda i, j: (0, 0)))
        args.append(norm_w.reshape(1, k).astype(F32))
    if residual is not None:
        in_specs.append(pl.BlockSpec((tm, tn), lambda i, j: (i, j)))
        args.append(residual)
    return pl.pallas_call(
        functools.partial(_matmul_kernel, tm=tm, norm=norm_w is not None,
                          residual=residual is not None, lay=lay),
        out_shape=jax.ShapeDtypeStruct((m, n), out_dtype),
        grid=(m // tm, n // tn),
        in_specs=in_specs,
        out_specs=pl.BlockSpec((tm, tn), lambda i, j: (i, j)),
        scratch_shapes=[pltpu.VMEM((tm, k), BF16)],
        compiler_params=_params("arbitrary", "arbitrary"),
        name=name,
    )(*args)


def _gdn_kernel(qkv_ref, z_ref, ba_ref, cw_ref, alog_ref, dtb_ref, nw_ref, o_ref,
                xbuf, s_scr, *, lay):
    c = pl.program_id(1)
    C = CHUNK
    width = 2 * GDN_QK + GDN_VW

    @pl.when(c == 0)
    def _():
        xbuf[0:SUBLANES, :] = jnp.zeros((SUBLANES, width), F32)
        s_scr[...] = jnp.zeros_like(s_scr)

    x = qkv_ref[...]
    xbuf[SUBLANES:SUBLANES + C, :] = x
    cw = cw_ref[...]
    y = x * cw[3:4, :]
    for d in range(1, GDN_CONV):
        y = y + xbuf[SUBLANES - d:SUBLANES - d + C, :] * cw[3 - d:4 - d, :]
    xbuf[0:SUBLANES, :] = x[C - SUBLANES:C, :]
    y = y * _sigmoid(y)

    valid = _valid_rows(c * C, C, lay)
    ba = ba_ref[...]
    beta_all = _sigmoid(ba)
    g_all = -jnp.exp(alog_ref[...]) * _softplus(ba + dtb_ref[...])
    g_all = jnp.where(valid, g_all, 0.0)

    row = lax.broadcasted_iota(jnp.int32, (C, C), 0)
    col = lax.broadcasted_iota(jnp.int32, (C, C), 1)
    eye = row == col
    causal = row >= col
    strict = row > col
    gc_all = _dot3(jnp.where(causal, 1.0, 0.0), g_all)

    heads = range(GDN_HEADS)
    l2 = lambda t: t * lax.rsqrt(jnp.sum(t * t, axis=-1, keepdims=True) + L2_EPS)
    q = [l2(y[:, h * GDN_DK:(h + 1) * GDN_DK]) * GDN_DK ** -0.5 for h in heads]
    k = [l2(y[:, GDN_QK + h * GDN_DK:GDN_QK + (h + 1) * GDN_DK]) for h in heads]
    v = [y[:, 2 * GDN_QK + h * GDN_DV:2 * GDN_QK + (h + 1) * GDN_DV] for h in heads]
    beta = [beta_all[:, h:h + 1] for h in heads]
    gc = [gc_all[:, GDN_HEADS + h:GDN_HEADS + h + 1] for h in heads]
    gc_last = [t[C - 1:C, :] for t in gc]
    decay = [jnp.where(causal, jnp.exp(jnp.where(causal, t - _col_to_row(t, eye), 0.0)), 0.0)
             for t in gc]
    egc = [jnp.exp(t) for t in gc]
    kb = [k[h] * beta[h] for h in heads]
    kq = [_dot(jnp.concatenate([kb[h], q[h]], axis=0), k[h], NT_DIMS) for h in heads]
    a = [jnp.where(strict, kq[h][0:C] * decay[h], 0.0) for h in heads]
    attn = [kq[h][C:2 * C] * decay[h] for h in heads]
    t_inv = _unit_lower_inverse(a, row, col)
    uw = [_dot(t_inv[h], jnp.concatenate([v[h] * beta[h], kb[h] * egc[h]], axis=1)) for h in heads]
    s = [s_scr[h] for h in heads]
    ws = [_dot(jnp.concatenate([uw[h][:, GDN_DV:], q[h] * egc[h]], axis=0), s[h]) for h in heads]
    v_new = [uw[h][:, 0:GDN_DV] - ws[h][0:C] for h in heads]
    o = [ws[h][C:2 * C] + _dot(attn[h], v_new[h]) for h in heads]
    upd = [_dot(k[h] * jnp.exp(gc_last[h] - gc[h]), v_new[h], TN_DIMS) for h in heads]
    z = z_ref[...]
    nw = nw_ref[...]
    for h in heads:
        s_scr[h] = s[h] * jnp.exp(gc_last[h]) + upd[h]
        oh = o[h] * lax.rsqrt(jnp.mean(o[h] * o[h], axis=-1, keepdims=True) + NORM_EPS) * nw
        zh = z[:, h * GDN_DV:(h + 1) * GDN_DV]
        o_ref[:, h * GDN_DV:(h + 1) * GDN_DV] = (oh * (zh * _sigmoid(zh))).astype(o_ref.dtype)


def _gdn(pg, conv_w, a_log, dt_bias, norm_w, lay):
    C = CHUNK
    nc = lay.tp // C
    width = 2 * GDN_QK + GDN_VW
    pad = LANES - 2 * GDN_HEADS
    alog = jnp.pad(a_log.astype(F32), (GDN_HEADS, pad)).reshape(1, LANES)
    dtb = jnp.pad(dt_bias.astype(F32), (GDN_HEADS, pad)).reshape(1, LANES)
    rows = lambda b, c: b * nc + c
    return pl.pallas_call(
        functools.partial(_gdn_kernel, lay=lay),
        out_shape=jax.ShapeDtypeStruct((lay.rows, GDN_VW), BF16),
        grid=(lay.batch, nc),
        in_specs=[
            pl.BlockSpec((C, width), lambda b, c: (rows(b, c), 0)),
            pl.BlockSpec((C, GDN_VW), lambda b, c: (rows(b, c), width // GDN_VW)),
            pl.BlockSpec((C, LANES), lambda b, c: (rows(b, c), (width + GDN_VW) // LANES)),
            pl.BlockSpec((GDN_CONV, width), lambda b, c: (0, 0)),
            pl.BlockSpec((1, LANES), lambda b, c: (0, 0)),
            pl.BlockSpec((1, LANES), lambda b, c: (0, 0)),
            pl.BlockSpec((1, GDN_DV), lambda b, c: (0, 0)),
        ],
        out_specs=pl.BlockSpec((C, GDN_VW), lambda b, c: (rows(b, c), 0)),
        scratch_shapes=[pltpu.VMEM((C + SUBLANES, width), F32),
                        pltpu.VMEM((GDN_HEADS, GDN_DK, GDN_DV), F32)],
        compiler_params=_params("arbitrary", "arbitrary"),
        name="gdn",
    )(pg, pg, pg, conv_w.astype(F32), alog, dtb, norm_w.reshape(1, GDN_DV).astype(F32))


def _mla_prep_kernel(q_ref, kv_ref, kpe_ref, cos_ref, sin_ref, qo_ref, ko_ref, vo_ref, *, tm, lay):
    cos = cos_ref[...]
    sin = sin_ref[...]
    lane = lax.broadcasted_iota(jnp.int32, cos.shape, 1)
    half = MLA_D_ROPE // 2

    def rot(x):
        swapped = jnp.where(lane < half, pltpu.roll(x, LANES - half, 1), pltpu.roll(x, half, 1))
        return x * cos + swapped * sin

    bias_lane = lane == MLA_D_ROPE
    p = lax.rem(pl.program_id(0) * tm, lay.tp) + lax.broadcasted_iota(jnp.int32, (tm, 1), 0)
    k_bias = jnp.where(p < lay.front, MASK_BIAS, 0.0)
    scale = (MLA_D_NOPE + MLA_D_ROPE) ** -0.5 * math.log2(math.e)
    k_pe = jnp.where(bias_lane, k_bias, rot(kpe_ref[...])).astype(BF16)
    for h in range(MLA_HEADS):
        lo = h * MLA_DQK
        qo_ref[:, lo:lo + MLA_D_NOPE] = (q_ref[:, lo:lo + MLA_D_NOPE] * scale).astype(BF16)
        qo_ref[:, lo + MLA_D_NOPE:lo + MLA_DQK] = jnp.where(
            bias_lane, 1.0, rot(q_ref[:, lo + MLA_D_NOPE:lo + MLA_DQK]) * scale).astype(BF16)
        ko_ref[:, lo:lo + MLA_D_NOPE] = kv_ref[:, h * MLA_D_NOPE:(h + 1) * MLA_D_NOPE].astype(BF16)
        ko_ref[:, lo + MLA_D_NOPE:lo + MLA_DQK] = k_pe
    vo_ref[...] = kv_ref[:, MLA_HEADS * MLA_D_NOPE:].astype(BF16)


def _flash_kernel(q_ref, k_ref, v_ref, o_ref, m_scr, l_scr, acc_scr, *, tile):
    qi = pl.program_id(2)
    T = tile
    q = q_ref[...]
    m_scr[...] = jnp.full_like(m_scr, -jnp.inf)
    l_scr[...] = jnp.zeros_like(l_scr)
    acc_scr[...] = jnp.zeros_like(acc_scr)

    sub = ATTN_TILE
    rows = [slice(i * sub, (i + 1) * sub) for i in range(T // sub)]

    def step(ki, diagonal):
        start = pl.multiple_of(ki * T, T)
        k = k_ref[pl.ds(start, T), :]
        v = v_ref[pl.ds(start, T), :]
        s = [lax.dot_general(q[r], k, NT_DIMS, preferred_element_type=F32) for r in rows]
        if diagonal:
            col = lax.broadcasted_iota(jnp.int32, (sub, T), 1)
            row = lax.broadcasted_iota(jnp.int32, (sub, T), 0)
            s = [jnp.where(col <= row + r.start, x, -jnp.inf) for r, x in zip(rows, s)]
        m_prev = [m_scr[r] for r in rows]
        m_new = [jnp.maximum(mp, jnp.max(x, axis=-1, keepdims=True)) for mp, x in zip(m_prev, s)]
        p = [jnp.exp2(x - mn) for x, mn in zip(s, m_new)]
        alpha = [jnp.exp2(mp - mn) for mp, mn in zip(m_prev, m_new)]
        pv = [jnp.dot(x.astype(BF16), v, preferred_element_type=F32) for x in p]
        for r, x, al, mn, y in zip(rows, p, alpha, m_new, pv):
            l_scr[r] = al * l_scr[r] + jnp.sum(x, axis=-1, keepdims=True)
            acc_scr[r] = al * acc_scr[r] + y
            m_scr[r] = mn

    def body(ki, carry):
        step(ki, False)
        return carry

    lax.fori_loop(0, qi, body, 0)
    step(qi, True)
    o_ref[...] = (acc_scr[...] / l_scr[...]).astype(o_ref.dtype)


def _rope_tables(lay):
    pos = (jnp.arange(lay.tp) - lay.front).astype(F32)
    inv_freq = ROPE_THETA ** (-jnp.arange(0, MLA_D_ROPE, 2, dtype=F32) / MLA_D_ROPE)
    ang = pos[:, None] * inv_freq[None, :]
    cos, sin = jnp.cos(ang), jnp.sin(ang)
    zeros = jnp.zeros((lay.tp, LANES - MLA_D_ROPE), F32)
    return (jnp.concatenate([cos, cos, zeros], axis=1),
            jnp.concatenate([-sin, sin, zeros], axis=1))


def _mla(p_cq, p_ckv, p_kpe, q_norm, wq, kv_norm, wkv, tables, lay):
    qraw = _matmul(p_cq, wq, norm_w=q_norm, name="mla_q_up")
    kvraw = _matmul(p_ckv, wkv, norm_w=kv_norm, name="mla_kv_up")
    rows = lay.rows
    tm = ATTN_TILE
    nt = lay.tp // tm
    hq = MLA_HEADS * MLA_DQK
    hv = MLA_HEADS * MLA_DV
    cos, sin = tables
    q, k, v = pl.pallas_call(
        functools.partial(_mla_prep_kernel, tm=tm, lay=lay),
        out_shape=(jax.ShapeDtypeStruct((rows, hq), BF16),
                   jax.ShapeDtypeStruct((rows, hq), BF16),
                   jax.ShapeDtypeStruct((rows, hv), BF16)),
        grid=(rows // tm,),
        in_specs=[pl.BlockSpec((tm, hq), lambda i: (i, 0)),
                  pl.BlockSpec((tm, MLA_HEADS * (MLA_D_NOPE + MLA_DV)), lambda i: (i, 0)),
                  pl.BlockSpec((tm, LANES), lambda i: (i, 0)),
                  pl.BlockSpec((tm, LANES), lambda i: (i % nt, 0)),
                  pl.BlockSpec((tm, LANES), lambda i: (i % nt, 0))],
        out_specs=(pl.BlockSpec((tm, hq), lambda i: (i, 0)),
                   pl.BlockSpec((tm, hq), lambda i: (i, 0)),
                   pl.BlockSpec((tm, hv), lambda i: (i, 0))),
        compiler_params=_params("arbitrary"),
        name="mla_prep",
    )(qraw, kvraw, p_kpe, cos, sin)
    ta = _tile(lay.tp, FLASH_TILE, ATTN_TILE)
    na = lay.tp // ta
    return pl.pallas_call(
        functools.partial(_flash_kernel, tile=ta),
        out_shape=jax.ShapeDtypeStruct((rows, hv), BF16),
        grid=(lay.batch, MLA_HEADS, na),
        in_specs=[pl.BlockSpec((ta, MLA_DQK), lambda b, h, i: (b * na + i, h)),
                  pl.BlockSpec((lay.tp, MLA_DQK), lambda b, h, i: (b, h)),
                  pl.BlockSpec((lay.tp, MLA_DV), lambda b, h, i: (b, h))],
        out_specs=pl.BlockSpec((ta, MLA_DV), lambda b, h, i: (b * na + i, h)),
        scratch_shapes=[pltpu.VMEM((ta, 1), F32), pltpu.VMEM((ta, 1), F32),
                        pltpu.VMEM((ta, MLA_DV), F32)],
        compiler_params=_params("arbitrary", "arbitrary", "arbitrary"),
        name="mla_attention",
    )(q, k, v)


def _rwkv_kernel(x_ref, mu_ref, wl_ref, w0_ref, a0_ref, kk_ref, ka_ref, rk_ref, lnw_ref, lnb_ref,
                 o_ref, xbuf, s_scr, *, lay):
    c = pl.program_id(1)
    C = CHUNK
    W = RWKV_W
    N = RWKV_N

    @pl.when(c == 0)
    def _():
        xbuf[0:SUBLANES, :] = jnp.zeros((SUBLANES, RWKV_IN_PAD), F32)
        s_scr[...] = jnp.zeros_like(s_scr)

    x = x_ref[...]
    xbuf[SUBLANES:SUBLANES + C, :] = x
    prev = xbuf[SUBLANES - 1:SUBLANES - 1 + C, :]
    xbuf[0:SUBLANES, :] = x[C - SUBLANES:C, :]
    xm = x + (prev - x) * mu_ref[...]
    xm = jnp.where(_valid_rows(c * C, C, lay), xm, 0.0)
    r = xm[:, 0:W]
    k = xm[:, W:2 * W]
    v = xm[:, 2 * W:3 * W]
    lin = xm[:, 3 * W:]
    lane = lax.broadcasted_iota(jnp.int32, lin.shape, 1)
    d1 = RWKV_DECAY_LORA
    d2 = d1 + RWKV_A_LORA
    lin = jnp.where(lane < d1, jnp.tanh(lin),
                    jnp.where(lane < d2, lin, jnp.where(lane < RWKV_LORA, _sigmoid(lin), 0.0)))
    lo = _dot(lin, wl_ref[...])
    w_log = -_softplus(-(w0_ref[...] + lo[:, 0:W])) - 0.5
    lw = -jnp.exp(w_log)
    a = _sigmoid(a0_ref[...] + lo[:, W:2 * W])
    g = lo[:, 2 * W:3 * W]
    kk_raw = k * kk_ref[...]
    k2 = k * (1.0 + (a - 1.0) * ka_ref[...])
    rkr = r * k2 * rk_ref[...]

    row = lax.broadcasted_iota(jnp.int32, (C, C), 0)
    col = lax.broadcasted_iota(jnp.int32, (C, C), 1)
    causal = row >= col
    strict = row > col
    cum = _dot3(jnp.where(causal, 1.0, 0.0), lw)
    c_last = cum[C - 1:C, :]
    e_prev = jnp.exp(cum - lw)
    e_neg = jnp.exp(-cum)
    e_pos = jnp.exp(cum)
    e_rest = jnp.exp(c_last - cum)
    e_last = jnp.exp(c_last)
    lnw = lnw_ref[...]
    lnb = lnb_ref[...]

    heads = range(RWKV_HEADS)
    sl = [slice(h * N, (h + 1) * N) for h in heads]
    kk = [kk_raw[:, x] for x in sl]
    kk = [t * lax.rsqrt(jnp.sum(t * t, axis=-1, keepdims=True) + L2_EPS) for t in kk]
    vh = [v[:, x] for x in sl]
    b_vec = [kk[h] * a[:, sl[h]] for h in heads]
    ar = [jnp.concatenate([-kk[h] * e_prev[:, sl[h]], r[:, sl[h]] * e_pos[:, sl[h]]], axis=0)
          for h in heads]
    bk = [jnp.concatenate([b_vec[h] * e_neg[:, sl[h]], k2[:, sl[h]] * e_neg[:, sl[h]]], axis=0)
          for h in heads]
    big = [_dot(ar[h], bk[h], NT_DIMS) for h in heads]
    a_ab = [jnp.where(strict, t[0:C, 0:C], 0.0) for t in big]
    ak_rk = [jnp.concatenate([jnp.where(strict, t[0:C, C:2 * C], 0.0),
                              jnp.where(causal, t[C:2 * C, C:2 * C], 0.0)], axis=0) for t in big]
    r_b = [jnp.where(causal, t[C:2 * C, 0:C], 0.0) for t in big]
    t_inv = _unit_lower_inverse([-t for t in a_ab], row, col)
    s = [s_scr[h] for h in heads]
    ar_s = [_dot(ar[h], s[h], NT_DIMS) for h in heads]
    akv = [_dot(ak_rk[h], vh[h]) for h in heads]
    u = [_dot(t_inv[h], ar_s[h][0:C] + akv[h][0:C]) for h in heads]
    yh = [ar_s[h][C:2 * C] + akv[h][C:2 * C] + _dot(r_b[h], u[h]) for h in heads]
    upd = [_dot(jnp.concatenate([u[h], vh[h]], axis=0),
                jnp.concatenate([b_vec[h] * e_rest[:, sl[h]], k2[:, sl[h]] * e_rest[:, sl[h]]], axis=0),
                TN_DIMS) for h in heads]
    for h in heads:
        s_scr[h] = s[h] * e_last[:, sl[h]] + upd[h]
        mean = jnp.mean(yh[h], axis=-1, keepdims=True)
        cen = yh[h] - mean
        var = jnp.mean(cen * cen, axis=-1, keepdims=True)
        yn = cen * lax.rsqrt(var + RWKV_LN_EPS) * lnw[:, sl[h]] + lnb[:, sl[h]]
        bonus = jnp.sum(rkr[:, sl[h]], axis=-1, keepdims=True) * vh[h]
        o_ref[:, sl[h]] = ((yn + bonus) * g[:, sl[h]]).astype(o_ref.dtype)


def _rwkv(p_rw, mu, w_lora, w0, a0, k_k, k_a, r_k, ln_w, ln_b, lay):
    C = CHUNK
    nc = lay.tp // C
    vec = lambda t: t.reshape(1, RWKV_W).astype(F32)
    rows = lambda b, c: (b * nc + c, 0)
    const = lambda b, c: (0, 0)
    return pl.pallas_call(
        functools.partial(_rwkv_kernel, lay=lay),
        out_shape=jax.ShapeDtypeStruct((lay.rows, RWKV_W), BF16),
        grid=(lay.batch, nc),
        in_specs=[pl.BlockSpec((C, RWKV_IN_PAD), rows),
                  pl.BlockSpec((1, RWKV_IN_PAD), const),
                  pl.BlockSpec((RWKV_LORA_PAD, 3 * RWKV_W), const)]
                 + [pl.BlockSpec((1, RWKV_W), const)] * 7,
        out_specs=pl.BlockSpec((C, RWKV_W), rows),
        scratch_shapes=[pltpu.VMEM((C + SUBLANES, RWKV_IN_PAD), F32),
                        pltpu.VMEM((RWKV_HEADS, RWKV_N, RWKV_N), F32)],
        compiler_params=_params("arbitrary", "arbitrary"),
        name="rwkv7",
    )(p_rw, jnp.pad(mu.astype(F32), (0, RWKV_IN_PAD - RWKV_IN)).reshape(1, RWKV_IN_PAD), w_lora,
      vec(w0), vec(a0), vec(k_k), vec(k_a), vec(r_k), vec(ln_w), vec(ln_b))


def _rwkv_lora_weight(w2, a2, g2):
    w = jnp.zeros((RWKV_LORA_PAD, 3 * RWKV_W), F32)
    d1 = RWKV_DECAY_LORA
    d2 = d1 + RWKV_A_LORA
    w = w.at[0:d1, 0:RWKV_W].set(w2)
    w = w.at[d1:d2, RWKV_W:2 * RWKV_W].set(a2)
    w = w.at[d2:RWKV_LORA, 2 * RWKV_W:].set(g2)
    return w.astype(BF16)


def _merge_kernel(ya_ref, yb_ref, yc_ref, wb_ref, g0_ref, g1_ref, g2_ref, o_ref):
    acc = _sigmoid(g0_ref[...]) * jnp.dot(ya_ref[...], wb_ref[0], preferred_element_type=F32)
    acc += _sigmoid(g1_ref[...]) * jnp.dot(yb_ref[...], wb_ref[1], preferred_element_type=F32)
    acc += _sigmoid(g2_ref[...]) * jnp.dot(yc_ref[...], wb_ref[2], preferred_element_type=F32)
    o_ref[...] = acc.astype(o_ref.dtype)


def _merge(ya, yb, yc, w_branch, gates):
    rows = ya.shape[0]
    tm = _tile(rows, 512, SUBLANES)
    tn = 512
    nj = D_MODEL // tn
    y_spec = pl.BlockSpec((tm, BRANCH_W), lambda i, j: (i, 0))
    gate = lambda br: pl.BlockSpec((tm, tn), lambda i, j: (i, br * nj + j))
    return pl.pallas_call(
        _merge_kernel,
        out_shape=jax.ShapeDtypeStruct((rows, D_MODEL), BF16),
        grid=(rows // tm, nj),
        in_specs=[y_spec, y_spec, y_spec,
                  pl.BlockSpec((N_BRANCH, BRANCH_W, tn), lambda i, j: (0, 0, j)),
                  gate(0), gate(1), gate(2)],
        out_specs=pl.BlockSpec((tm, tn), lambda i, j: (i, j)),
        compiler_params=_params("arbitrary", "arbitrary"),
        name="gated_merge",
    )(ya, yb, yc, w_branch, gates, gates, gates)


def _router_kernel(h_ref, nw_ref, wr_ref, br_ref, xt_ref, idx_ref, gw_ref, *, tm, lay):
    i = pl.program_id(0)
    x = h_ref[...]
    x = x * lax.rsqrt(jnp.mean(x * x, axis=-1, keepdims=True) + NORM_EPS) * nw_ref[...]
    x = jnp.where(_valid_rows(lax.rem(i * tm, lay.tp), tm, lay), x, 0.0)
    xt_ref[...] = x
    logits = _dot3(x, wr_ref[...]) + br_ref[...]
    lane = lax.broadcasted_iota(jnp.int32, logits.shape, 1)
    neg = -jnp.inf

    def top1(vals):
        m = jnp.max(vals, axis=-1, keepdims=True)
        idx = jnp.min(jnp.where(vals == m, lane, LANES), axis=-1, keepdims=True)
        return m, idx

    g_mask = lane < N_GROUPS
    g_top, g_idx = top1(jnp.where(g_mask, logits, neg))
    p_group = 1.0 / jnp.sum(jnp.where(g_mask, jnp.exp(logits - g_top), 0.0), axis=-1, keepdims=True)
    e_lo = N_GROUPS + g_idx * EXPERTS_PER_GROUP
    e_vals = jnp.where((lane >= e_lo) & (lane < e_lo + EXPERTS_PER_GROUP), logits, neg)
    e1, i1 = top1(e_vals)
    e2, i2 = top1(jnp.where(lane == i1, neg, e_vals))
    t = jnp.exp(e2 - e1)
    w1 = p_group / (1.0 + t)
    w2 = p_group * t / (1.0 + t)
    idx_ref[...] = jnp.where(lane == 0, i1 - N_GROUPS, jnp.where(lane == 1, i2 - N_GROUPS, 0))
    gw_ref[...] = jnp.where(lane == 0, w1, jnp.where(lane == 1, w2, 0.0))


def _router(h, norm_w, w_r, b_r, lay):
    rows = h.shape[0]
    tm = _tile(rows, 512, SUBLANES)
    row_spec = lambda w: pl.BlockSpec((tm, w), lambda i: (i, 0))
    return pl.pallas_call(
        functools.partial(_router_kernel, tm=tm, lay=lay),
        out_shape=(jax.ShapeDtypeStruct((rows, D_MODEL), F32),
                   jax.ShapeDtypeStruct((rows, LANES), jnp.int32),
                   jax.ShapeDtypeStruct((rows, LANES), F32)),
        grid=(rows // tm,),
        in_specs=[row_spec(D_MODEL),
                  pl.BlockSpec((1, D_MODEL), lambda i: (0, 0)),
                  pl.BlockSpec((D_MODEL, LANES), lambda i: (0, 0)),
                  pl.BlockSpec((1, LANES), lambda i: (0, 0))],
        out_specs=(row_spec(D_MODEL), row_spec(LANES), row_spec(LANES)),
        compiler_params=_params("arbitrary"),
        name="moe_router",
    )(h, norm_w.reshape(1, D_MODEL).astype(F32), w_r, b_r)


def _row_gather(idx_ref, base, n, src_hbm, dst, sem):
    def body(r, carry):
        tok = idx_ref[base + r]
        pltpu.make_async_copy(src_hbm.at[pl.ds(tok, 1), :], dst.at[pl.ds(r, 1), :], sem).start()
        return carry
    lax.fori_loop(0, n, body, 0, unroll=GATHER_UNROLL)


def _row_gather_wait(n, src_hbm, dst, sem):
    def body(r, carry):
        pltpu.make_async_copy(src_hbm.at[pl.ds(0, 1), :], dst.at[pl.ds(r, 1), :], sem).wait()
        return carry
    lax.fori_loop(0, n, body, 0, unroll=GATHER_UNROLL)


def _expert_kernel(be_ref, tok_ref, nused_ref, xt_hbm, wgu_ref, wd_ref, o_ref, xbuf, sem):
    i = pl.program_id(0)
    nb = pl.num_programs(0)
    tb = MOE_TILE
    slot = i % 2
    used = nused_ref[0]

    @pl.when((i == 0) & (used > 0))
    def _():
        _row_gather(tok_ref, 0, tb, xt_hbm, xbuf.at[0], sem.at[0])

    @pl.when((i + 1 < nb) & (i + 1 < used))
    def _():
        _row_gather(tok_ref, (i + 1) * tb, tb, xt_hbm, xbuf.at[1 - slot], sem.at[1 - slot])

    @pl.when(i < used)
    def _():
        _row_gather_wait(tb, xt_hbm, xbuf.at[slot], sem.at[slot])
        x = xbuf[slot].astype(BF16)
        gu = jnp.dot(x, wgu_ref[0], preferred_element_type=F32)
        gate = gu[:, :D_EXPERT]
        act = (gate * _sigmoid(gate) * gu[:, D_EXPERT:]).astype(BF16)
        o_ref[...] = jnp.dot(act, wd_ref[0], preferred_element_type=F32)

    @pl.when(i >= used)
    def _():
        o_ref[...] = jnp.zeros_like(o_ref)


def _experts(xt, block_expert, row_token, n_used, w_gu, w_down):
    n_rows = row_token.shape[0]
    nb = n_rows // MOE_TILE
    grid_spec = pltpu.PrefetchScalarGridSpec(
        num_scalar_prefetch=3,
        grid=(nb,),
        in_specs=[pl.BlockSpec(memory_space=pl.ANY),
                  pl.BlockSpec((1, D_MODEL, 2 * D_EXPERT), lambda i, be, tok, nu: (be[i], 0, 0)),
                  pl.BlockSpec((1, D_EXPERT, D_MODEL), lambda i, be, tok, nu: (be[i], 0, 0))],
        out_specs=pl.BlockSpec((MOE_TILE, D_MODEL), lambda i, be, tok, nu: (i, 0)),
        scratch_shapes=[pltpu.VMEM((2, MOE_TILE, D_MODEL), F32),
                        pltpu.SemaphoreType.DMA((2,))],
    )
    return pl.pallas_call(
        _expert_kernel,
        out_shape=jax.ShapeDtypeStruct((n_rows, D_MODEL), F32),
        grid_spec=grid_spec,
        compiler_params=_params("arbitrary"),
        name="moe_experts",
    )(block_expert, row_token, n_used, xt, w_gu, w_down)


def _combine_kernel(dest0_ref, dest1_ref, y_hbm, h_ref, gw_ref, o_ref, buf0, buf1, sem):
    i = pl.program_id(0)
    nb = pl.num_programs(0)
    tc = COMBINE_TILE
    slot = i % 2

    def start(step, s):
        _row_gather(dest0_ref, step * tc, tc, y_hbm, buf0.at[s], sem.at[0, s])
        _row_gather(dest1_ref, step * tc, tc, y_hbm, buf1.at[s], sem.at[1, s])

    @pl.when(i == 0)
    def _():
        start(0, 0)

    @pl.when(i + 1 < nb)
    def _():
        start(i + 1, 1 - slot)

    _row_gather_wait(tc, y_hbm, buf0.at[slot], sem.at[0, slot])
    _row_gather_wait(tc, y_hbm, buf1.at[slot], sem.at[1, slot])
    gw = gw_ref[...]
    o_ref[...] = h_ref[...] + gw[:, 0:1] * buf0[slot] + gw[:, 1:2] * buf1[slot]


def _combine(h, y_rows, dest0, dest1, gw):
    rows = h.shape[0]
    tc = COMBINE_TILE
    grid_spec = pltpu.PrefetchScalarGridSpec(
        num_scalar_prefetch=2,
        grid=(rows // tc,),
        in_specs=[pl.BlockSpec(memory_space=pl.ANY),
                  pl.BlockSpec((tc, D_MODEL), lambda i, d0, d1: (i, 0)),
                  pl.BlockSpec((tc, LANES), lambda i, d0, d1: (i, 0))],
        out_specs=pl.BlockSpec((tc, D_MODEL), lambda i, d0, d1: (i, 0)),
        scratch_shapes=[pltpu.VMEM((2, tc, D_MODEL), F32),
                        pltpu.VMEM((2, tc, D_MODEL), F32),
                        pltpu.SemaphoreType.DMA((2, 2))],
    )
    return pl.pallas_call(
        _combine_kernel,
        out_shape=jax.ShapeDtypeStruct((rows, D_MODEL), F32),
        grid_spec=grid_spec,
        compiler_params=_params("arbitrary"),
        name="moe_combine",
    )(dest0, dest1, y_rows, h, gw)


def _expert_ranks(expert):
    n = expert.shape[0]
    blk = 2 * MOE_TILE
    onehot = (expert[:, None] == jnp.arange(N_EXPERTS, dtype=jnp.int32)[None, :]).astype(F32)
    oh = onehot.reshape(n // blk, blk, N_EXPERTS)
    within = jnp.einsum("ts,bse->bte", jnp.tril(jnp.ones((blk, blk), F32)), oh)
    totals = within[:, -1, :]
    before = jnp.cumsum(totals, axis=0) - totals
    csum = (within + before[:, None, :]).reshape(n, N_EXPERTS)
    rank = jnp.sum(csum * onehot, axis=1).astype(jnp.int32) - 1
    return rank, csum[-1].astype(jnp.int32)


def _moe(h, norm_w, w_r, b_r, w_gu, w_down, lay):
    rows = h.shape[0]
    xt, idx, gw = _router(h, norm_w, w_r, b_r, lay)
    expert = idx[:, :TOP_K].reshape(-1)
    n_assign = rows * TOP_K
    rank, counts = _expert_ranks(expert)
    padded = (counts + MOE_TILE - 1) // MOE_TILE * MOE_TILE
    padded_end = jnp.cumsum(padded)
    padded_start = padded_end - padded
    dest = (padded_start[expert] + rank).astype(jnp.int32)
    n_rows = (-(-n_assign // MOE_TILE) + N_EXPERTS) * MOE_TILE
    nb = n_rows // MOE_TILE
    row_token = jnp.zeros((n_rows,), jnp.int32).at[dest].set(
        jnp.arange(n_assign, dtype=jnp.int32) // TOP_K, unique_indices=True)
    block_expert = jnp.minimum(
        jnp.searchsorted(padded_end, jnp.arange(nb, dtype=jnp.int32) * MOE_TILE, side="right"),
        N_EXPERTS - 1).astype(jnp.int32)
    n_used = (padded_end[-1:] // MOE_TILE).astype(jnp.int32)
    y_rows = _experts(xt, block_expert, row_token, n_used, w_gu, w_down)
    dest2 = dest.reshape(rows, TOP_K)
    return _combine(h, y_rows, dest2[:, 0], dest2[:, 1], gw)


def _final_norm_kernel(h_ref, w_ref, o_ref):
    x = h_ref[...]
    o_ref[...] = x * lax.rsqrt(jnp.mean(x * x, axis=-1, keepdims=True) + NORM_EPS) * w_ref[...]


def _final_norm(h, w, lay, seq):
    first = lay.front + N_META
    tm = math.gcd(math.gcd(first, seq), lay.tp)
    per_b = lay.tp // tm
    nseq = seq // tm
    return pl.pallas_call(
        _final_norm_kernel,
        out_shape=jax.ShapeDtypeStruct((lay.batch * seq, D_MODEL), F32),
        grid=(lay.batch, nseq),
        in_specs=[pl.BlockSpec((tm, D_MODEL), lambda b, i: (b * per_b + first // tm + i, 0)),
                  pl.BlockSpec((1, D_MODEL), lambda b, i: (0, 0))],
        out_specs=pl.BlockSpec((tm, D_MODEL), lambda b, i: (b * nseq + i, 0)),
        compiler_params=_params("arbitrary", "arbitrary"),
        name="final_norm",
    )(h, w.reshape(1, D_MODEL).astype(F32))


def _pad_cols(w, width):
    return jnp.pad(w, ((0, 0), (0, width - w.shape[1])))


def _layer_weights(w_in, mla_w_q_up, mla_w_kv_up, w2, a2, g2, w_branch, w_out,
                   w_rg, b_rg, w_re, b_re, w_gate, w_up, w_down):
    offs = [0]
    for s in IN_SPLITS:
        offs.append(offs[-1] + s)
    seg = lambda i, j: w_in[:, offs[i]:offs[j]]
    w_gdn = jnp.concatenate([seg(0, 4), _pad_cols(seg(4, 6), LANES)], axis=1).astype(BF16)
    w_cq = seg(6, 7).astype(BF16)
    w_ckv = seg(7, 8).astype(BF16)
    w_kpe = _pad_cols(seg(8, 9), LANES).astype(BF16)
    w_rw = _pad_cols(seg(9, 10), RWKV_IN_PAD).astype(BF16)
    w_gates = seg(10, 11).astype(BF16)
    wq = mla_w_q_up.reshape(MLA_Q_LORA, MLA_HEADS, MLA_D_NOPE + MLA_D_ROPE)
    wq = jnp.pad(wq, ((0, 0), (0, 0), (0, MLA_DQK - MLA_D_NOPE - MLA_D_ROPE)))
    wq = wq.reshape(MLA_Q_LORA, MLA_HEADS * MLA_DQK).astype(BF16)
    wkv = mla_w_kv_up.reshape(MLA_KV_LORA, MLA_HEADS, MLA_D_NOPE + MLA_DV)
    wkv = jnp.concatenate([wkv[:, :, :MLA_D_NOPE].reshape(MLA_KV_LORA, -1),
                           wkv[:, :, MLA_D_NOPE:].reshape(MLA_KV_LORA, -1)], axis=1).astype(BF16)
    w_r = _pad_cols(jnp.concatenate([w_rg, w_re], axis=1), LANES).astype(F32)
    b_r = jnp.pad(jnp.concatenate([b_rg, b_re]), (0, LANES - N_GROUPS - N_EXPERTS)).reshape(1, LANES)
    return dict(
        w_gdn=w_gdn, w_cq=w_cq, w_ckv=w_ckv, w_kpe=w_kpe, w_rw=w_rw, w_gates=w_gates,
        wq=wq, wkv=wkv, w_lora=_rwkv_lora_weight(w2, a2, g2),
        w_branch=w_branch.astype(BF16), w_out=w_out.astype(BF16), w_r=w_r, b_r=b_r.astype(F32),
        w_gu=jnp.concatenate([w_gate, w_up], axis=2).astype(BF16), w_down=w_down.astype(BF16))


def _mixer(h, norm_w, lw, gdn_p, mla_p, rwkv_p, tables, lay):
    proj = lambda w, name: _matmul(h, w, norm_w=norm_w, lay=lay, name=name)
    ya = _gdn(proj(lw["w_gdn"], "proj_gdn"), *gdn_p, lay)
    yb = _mla(proj(lw["w_cq"], "proj_cq"), proj(lw["w_ckv"], "proj_ckv"), proj(lw["w_kpe"], "proj_kpe"),
              mla_p[0], lw["wq"], mla_p[1], lw["wkv"], tables, lay)
    yc = _rwkv(proj(lw["w_rw"], "proj_rwkv"), rwkv_p[0], lw["w_lora"], *rwkv_p[1:], lay)
    merged = _merge(ya, yb, yc, lw["w_branch"], proj(lw["w_gates"], "proj_gates"))
    return _matmul(merged, lw["w_out"], residual=h, name="mixer_out")


def kernel(x, meta_tokens, norm_mix, w_in, gdn_conv, gdn_a_log, gdn_dt_bias, gdn_norm, mla_q_norm, mla_w_q_up, mla_kv_norm, mla_w_kv_up, rwkv_mu, rwkv_w0, rwkv_w2, rwkv_a0, rwkv_a2, rwkv_g2, rwkv_k_k, rwkv_k_a, rwkv_r_k, rwkv_ln_w, rwkv_ln_b, w_branch, w_out, norm_ffn, w_router_group, b_router_group, w_router_expert, b_router_expert, w_exp_gate, w_exp_up, w_exp_down, norm_final):
    batch, seq, _ = x.shape
    depth = w_in.shape[0]
    lay = make_layout(batch, seq)
    meta = jnp.broadcast_to(meta_tokens[None].astype(x.dtype), (batch, N_META, D_MODEL))
    h = jnp.concatenate([jnp.zeros((batch, lay.front, D_MODEL), x.dtype), meta, x,
                         jnp.zeros((batch, lay.tp - lay.front - lay.t, D_MODEL), x.dtype)], axis=1)
    h = h.reshape(lay.rows, D_MODEL)
    tables = _rope_tables(lay)
    for l in range(depth):
        lw = _layer_weights(w_in[l], mla_w_q_up[l], mla_w_kv_up[l], rwkv_w2[l], rwkv_a2[l], rwkv_g2[l],
                            w_branch[l], w_out[l], w_router_group[l], b_router_group[l],
                            w_router_expert[l], b_router_expert[l],
                            w_exp_gate[l], w_exp_up[l], w_exp_down[l])
        h = _mixer(h, norm_mix[l], lw,
                   (gdn_conv[l], gdn_a_log[l], gdn_dt_bias[l], gdn_norm[l]),
                   (mla_q_norm[l], mla_kv_norm[l]),
                   (rwkv_mu[l], rwkv_w0[l], rwkv_a0[l], rwkv_k_k[l], rwkv_k_a[l],
                    rwkv_r_k[l].reshape(-1), rwkv_ln_w[l], rwkv_ln_b[l]),
                   tables, lay)
        h = _moe(h, norm_ffn[l], lw["w_r"], lw["b_r"], lw["w_gu"], lw["w_down"], lay)
    return _final_norm(h, norm_final, lay, seq).reshape(batch, seq, D_MODEL)
```

```python
import functools
import math
from typing import NamedTuple

import jax
import jax.numpy as jnp
from jax import lax
from jax.experimental import pallas as pl
from jax.experimental.pallas import tpu as pltpu

F32 = jnp.float32
BF16 = jnp.bfloat16

D_MODEL = 2048
N_META = 16
NORM_EPS = 1e-6
L2_EPS = 1e-6

GDN_HEADS = 8
GDN_DK = 128
GDN_DV = 128
GDN_CONV = 4
GDN_QK = GDN_HEADS * GDN_DK
GDN_VW = GDN_HEADS * GDN_DV

MLA_HEADS = 8
MLA_Q_LORA = 768
MLA_KV_LORA = 512
MLA_D_NOPE = 128
MLA_D_ROPE = 64
MLA_DV = 128
MLA_DQK = 256
ROPE_THETA = 10000.0

RWKV_HEADS = 16
RWKV_N = 64
RWKV_W = RWKV_HEADS * RWKV_N
RWKV_DECAY_LORA = 64
RWKV_A_LORA = 64
RWKV_G_LORA = 160
RWKV_LORA = RWKV_DECAY_LORA + RWKV_A_LORA + RWKV_G_LORA
RWKV_LORA_PAD = 384
RWKV_LN_EPS = 64e-5
RWKV_IN = 3 * RWKV_W + RWKV_LORA
RWKV_IN_PAD = 3 * RWKV_W + RWKV_LORA_PAD

N_BRANCH = 3
BRANCH_W = 1024

N_GROUPS = 4
EXPERTS_PER_GROUP = 8
N_EXPERTS = N_GROUPS * EXPERTS_PER_GROUP
TOP_K = 2
D_EXPERT = 512

IN_SPLITS = (GDN_QK, GDN_QK, GDN_VW, GDN_VW, GDN_HEADS, GDN_HEADS,
             MLA_Q_LORA, MLA_KV_LORA, MLA_D_ROPE, RWKV_IN, N_BRANCH * D_MODEL)

LANES = 128
SUBLANES = 8
CHUNK = 64
ATTN_TILE = 256
FLASH_TILE = 768
MASK_BIAS = -1e30
MOE_TILE = 256
COMBINE_TILE = 128
GATHER_UNROLL = 8
VMEM_LIMIT = 48 * 1024 * 1024

NT_DIMS = (((1,), (1,)), ((), ()))
TN_DIMS = (((0,), (0,)), ((), ()))


class Layout(NamedTuple):
    batch: int
    t: int
    front: int
    tp: int

    @property
    def rows(self):
        return self.batch * self.tp


def make_layout(batch, seq):
    t = N_META + seq
    front = (-N_META) % CHUNK
    tp = -(-(front + t) // ATTN_TILE) * ATTN_TILE
    return Layout(batch, t, front, tp)


def _tile(n, pref, align):
    best = None
    for cand in range(align, min(n, pref) + 1, align):
        if n % cand == 0:
            best = cand
    if best is None:
        raise ValueError(f"no tile for {n}")
    return best


def _params(*sem):
    return pltpu.CompilerParams(dimension_semantics=sem, vmem_limit_bytes=VMEM_LIMIT)


def _valid_rows(p0, n, lay):
    p = p0 + lax.broadcasted_iota(jnp.int32, (n, 1), 0)
    p = jnp.where(p >= lay.tp, p - lay.tp, p)
    return (p >= lay.front) & (p < lay.front + lay.t)


def _split_bf16(a):
    hi = a.astype(BF16)
    lo = (a - hi.astype(F32)).astype(BF16)
    return hi, lo


def _dot(a, b, dims=None):
    a = a.astype(BF16)
    b = b.astype(BF16)
    if dims is None:
        return jnp.dot(a, b, preferred_element_type=F32)
    return lax.dot_general(a, b, dims, preferred_element_type=F32)


def _dot3(a, b):
    ah, al = _split_bf16(a)
    bh, bl = _split_bf16(b)
    return (jnp.dot(ah, bh, preferred_element_type=F32)
            + jnp.dot(al, bh, preferred_element_type=F32)
            + jnp.dot(ah, bl, preferred_element_type=F32))


INV_DOT = _dot


def _sigmoid(x):
    return 1.0 / (1.0 + jnp.exp(-x))


def _softplus(x):
    return jnp.maximum(x, 0.0) + jnp.log(1.0 + jnp.exp(-jnp.abs(x)))


def _unit_lower_inverse(a_list, row, col):
    blk16 = (row >> 4) == (col >> 4)
    blk32 = (row >> 5) == (col >> 5)
    eye = jnp.where(row == col, 1.0, 0.0)
    p = [jnp.where(blk16, -a, 0.0) for a in a_list]
    t = [eye + n for n in p]
    for _ in range(3):
        p = [INV_DOT(x, x) for x in p]
        t = [ti + INV_DOT(ti, pi) for ti, pi in zip(t, p)]
    e1 = [jnp.where(blk32 & jnp.logical_not(blk16), a, 0.0) for a in a_list]
    te = [INV_DOT(ti, ei) for ti, ei in zip(t, e1)]
    t = [ti - INV_DOT(tei, ti) for ti, tei in zip(t, te)]
    e2 = [jnp.where(blk32, 0.0, a) for a in a_list]
    te = [INV_DOT(ti, ei) for ti, ei in zip(t, e2)]
    return [ti - INV_DOT(tei, ti) for ti, tei in zip(t, te)]


def _col_to_row(col_vec, eye):
    return jnp.sum(jnp.where(eye, col_vec, 0.0), axis=0, keepdims=True)


def _matmul_kernel(*refs, tm, norm, residual, lay):
    it = iter(refs)
    a_ref = next(it)
    b_ref = next(it)
    nw_ref = next(it) if norm else None
    r_ref = next(it) if residual else None
    o_ref = next(it)
    a_scr = next(it)
    i = pl.program_id(0)

    @pl.when(pl.program_id(1) == 0)
    def _():
        a = a_ref[...].astype(F32)
        if norm:
            ms = jnp.mean(a * a, axis=-1, keepdims=True)
            a = a * lax.rsqrt(ms + NORM_EPS) * nw_ref[...]
        if lay is not None:
            a = jnp.where(_valid_rows(lax.rem(i * tm, lay.tp), tm, lay), a, 0.0)
        a_scr[...] = a.astype(BF16)

    acc = jnp.dot(a_scr[...], b_ref[...], preferred_element_type=F32)
    if residual:
        acc = acc + r_ref[...]
    o_ref[...] = acc.astype(o_ref.dtype)


def _matmul(a, b, *, norm_w=None, residual=None, lay=None, out_dtype=F32, name="matmul"):
    m, k = a.shape
    n = b.shape[1]
    tm = _tile(m, 512, SUBLANES)
    tn = _tile(n, 512, LANES)
    in_specs = [pl.BlockSpec((tm, k), lambda i, j: (i, 0)),
                pl.BlockSpec((k, tn), lambda i, j: (0, j))]
    args = [a, b]
    if norm_w is not None:
        in_specs.append(pl.BlockSpec((1, k), lambda i, j: (0, 0)))
        args.append(norm_w.reshape(1, k).astype(F32))
    if residual is not None:
        in_specs.append(pl.BlockSpec((tm, tn), lambda i, j: (i, j)))
        args.append(residual)
    return pl.pallas_call(
        functools.partial(_matmul_kernel, tm=tm, norm=norm_w is not None,
                          residual=residual is not None, lay=lay),
        out_shape=jax.ShapeDtypeStruct((m, n), out_dtype),
        grid=(m // tm, n // tn),
        in_specs=in_specs,
        out_specs=pl.BlockSpec((tm, tn), lambda i, j: (i, j)),
        scratch_shapes=[pltpu.VMEM((tm, k), BF16)],
        compiler_params=_params("arbitrary", "arbitrary"),
        name=name,
    )(*args)


def _gdn_kernel(qkv_ref, z_ref, ba_ref, cw_ref, alog_ref, dtb_ref, nw_ref, o_ref,
                xbuf, s_scr, *, lay):
    c = pl.program_id(1)
    C = CHUNK
    width = 2 * GDN_QK + GDN_VW

    @pl.when(c == 0)
    def _():
        xbuf[0:SUBLANES, :] = jnp.zeros((SUBLANES, width), F32)
        s_scr[...] = jnp.zeros_like(s_scr)

    x = qkv_ref[...]
    xbuf[SUBLANES:SUBLANES + C, :] = x
    cw = cw_ref[...]
    y = x * cw[3:4, :]
    for d in range(1, GDN_CONV):
        y = y + xbuf[SUBLANES - d:SUBLANES - d + C, :] * cw[3 - d:4 - d, :]
    xbuf[0:SUBLANES, :] = x[C - SUBLANES:C, :]
    y = y * _sigmoid(y)

    valid = _valid_rows(c * C, C, lay)
    ba = ba_ref[...]
    beta_all = _sigmoid(ba)
    g_all = -jnp.exp(alog_ref[...]) * _softplus(ba + dtb_ref[...])
    g_all = jnp.where(valid, g_all, 0.0)

    row = lax.broadcasted_iota(jnp.int32, (C, C), 0)
    col = lax.broadcasted_iota(jnp.int32, (C, C), 1)
    eye = row == col
    causal = row >= col
    strict = row > col
    gc_all = _dot3(jnp.where(causal, 1.0, 0.0), g_all)

    heads = range(GDN_HEADS)
    l2 = lambda t: t * lax.rsqrt(jnp.sum(t * t, axis=-1, keepdims=True) + L2_EPS)
    q = [l2(y[:, h * GDN_DK:(h + 1) * GDN_DK]) * GDN_DK ** -0.5 for h in heads]
    k = [l2(y[:, GDN_QK + h * GDN_DK:GDN_QK + (h + 1) * GDN_DK]) for h in heads]
    v = [y[:, 2 * GDN_QK + h * GDN_DV:2 * GDN_QK + (h + 1) * GDN_DV] for h in heads]
    beta = [beta_all[:, h:h + 1] for h in heads]
    gc = [gc_all[:, GDN_HEADS + h:GDN_HEADS + h + 1] for h in heads]
    gc_last = [t[C - 1:C, :] for t in gc]
    decay = [jnp.where(causal, jnp.exp(jnp.where(causal, t - _col_to_row(t, eye), 0.0)), 0.0)
             for t in gc]
    egc = [jnp.exp(t) for t in gc]
    kb = [k[h] * beta[h] for h in heads]
    kq = [_dot(jnp.concatenate([kb[h], q[h]], axis=0), k[h], NT_DIMS) for h in heads]
    a = [jnp.where(strict, kq[h][0:C] * decay[h], 0.0) for h in heads]
    attn = [kq[h][C:2 * C] * decay[h] for h in heads]
    t_inv = _unit_lower_inverse(a, row, col)
    uw = [_dot(t_inv[h], jnp.concatenate([v[h] * beta[h], kb[h] * egc[h]], axis=1)) for h in heads]
    s = [s_scr[h] for h in heads]
    ws = [_dot(jnp.concatenate([uw[h][:, GDN_DV:], q[h] * egc[h]], axis=0), s[h]) for h in heads]
    v_new = [uw[h][:, 0:GDN_DV] - ws[h][0:C] for h in heads]
    o = [ws[h][C:2 * C] + _dot(attn[h], v_new[h]) for h in heads]
    upd = [_dot(k[h] * jnp.exp(gc_last[h] - gc[h]), v_new[h], TN_DIMS) for h in heads]
    z = z_ref[...]
    nw = nw_ref[...]
    for h in heads:
        s_scr[h] = s[h] * jnp.exp(gc_last[h]) + upd[h]
        oh = o[h] * lax.rsqrt(jnp.mean(o[h] * o[h], axis=-1, keepdims=True) + NORM_EPS) * nw
        zh = z[:, h * GDN_DV:(h + 1) * GDN_DV]
        o_ref[:, h * GDN_DV:(h + 1) * GDN_DV] = (oh * (zh * _sigmoid(zh))).astype(o_ref.dtype)


def _gdn(pg, conv_w, a_log, dt_bias, norm_w, lay):
    C = CHUNK
    nc = lay.tp // C
    width = 2 * GDN_QK + GDN_VW
    pad = LANES - 2 * GDN_HEADS
    alog = jnp.pad(a_log.astype(F32), (GDN_HEADS, pad)).reshape(1, LANES)
    dtb = jnp.pad(dt_bias.astype(F32), (GDN_HEADS, pad)).reshape(1, LANES)
    rows = lambda b, c: b * nc + c
    return pl.pallas_call(
        functools.partial(_gdn_kernel, lay=lay),
        out_shape=jax.ShapeDtypeStruct((lay.rows, GDN_VW), BF16),
        grid=(lay.batch, nc),
        in_specs=[
            pl.BlockSpec((C, width), lambda b, c: (rows(b, c), 0)),
            pl.BlockSpec((C, GDN_VW), lambda b, c: (rows(b, c), width // GDN_VW)),
            pl.BlockSpec((C, LANES), lambda b, c: (rows(b, c), (width + GDN_VW) // LANES)),
            pl.BlockSpec((GDN_CONV, width), lambda b, c: (0, 0)),
            pl.BlockSpec((1, LANES), lambda b, c: (0, 0)),
            pl.BlockSpec((1, LANES), lambda b, c: (0, 0)),
            pl.BlockSpec((1, GDN_DV), lambda b, c: (0, 0)),
        ],
        out_specs=pl.BlockSpec((C, GDN_VW), lambda b, c: (rows(b, c), 0)),
        scratch_shapes=[pltpu.VMEM((C + SUBLANES, width), F32),
                        pltpu.VMEM((GDN_HEADS, GDN_DK, GDN_DV), F32)],
        compiler_params=_params("arbitrary", "arbitrary"),
        name="gdn",
    )(pg, pg, pg, conv_w.astype(F32), alog, dtb, norm_w.reshape(1, GDN_DV).astype(F32))


def _mla_prep_kernel(q_ref, kv_ref, kpe_ref, cos_ref, sin_ref, qo_ref, ko_ref, vo_ref, *, tm, lay):
    cos = cos_ref[...]
    sin = sin_ref[...]
    lane = lax.broadcasted_iota(jnp.int32, cos.shape, 1)
    half = MLA_D_ROPE // 2

    def rot(x):
        swapped = jnp.where(lane < half, pltpu.roll(x, LANES - half, 1), pltpu.roll(x, half, 1))
        return x * cos + swapped * sin

    bias_lane = lane == MLA_D_ROPE
    p = lax.rem(pl.program_id(0) * tm, lay.tp) + lax.broadcasted_iota(jnp.int32, (tm, 1), 0)
    k_bias = jnp.where(p < lay.front, MASK_BIAS, 0.0)
    scale = (MLA_D_NOPE + MLA_D_ROPE) ** -0.5 * math.log2(math.e)
    k_pe = jnp.where(bias_lane, k_bias, rot(kpe_ref[...])).astype(BF16)
    for h in range(MLA_HEADS):
        lo = h * MLA_DQK
        qo_ref[:, lo:lo + MLA_D_NOPE] = (q_ref[:, lo:lo + MLA_D_NOPE] * scale).astype(BF16)
        qo_ref[:, lo + MLA_D_NOPE:lo + MLA_DQK] = jnp.where(
            bias_lane, 1.0, rot(q_ref[:, lo + MLA_D_NOPE:lo + MLA_DQK]) * scale).astype(BF16)
        ko_ref[:, lo:lo + MLA_D_NOPE] = kv_ref[:, h * MLA_D_NOPE:(h + 1) * MLA_D_NOPE].astype(BF16)
        ko_ref[:, lo + MLA_D_NOPE:lo + MLA_DQK] = k_pe
    vo_ref[...] = kv_ref[:, MLA_HEADS * MLA_D_NOPE:].astype(BF16)


def _flash_kernel(q_ref, k_ref, v_ref, o_ref, m_scr, l_scr, acc_scr, *, tile):
    qi = pl.program_id(2)
    T = tile
    q = q_ref[...]
    m_scr[...] = jnp.full_like(m_scr, -jnp.inf)
    l_scr[...] = jnp.zeros_like(l_scr)
    acc_scr[...] = jnp.zeros_like(acc_scr)

    sub = ATTN_TILE
    rows = [slice(i * sub, (i + 1) * sub) for i in range(T // sub)]

    def step(ki, diagonal):
        start = pl.multiple_of(ki * T, T)
        k = k_ref[pl.ds(start, T), :]
        v = v_ref[pl.ds(start, T), :]
        s = [lax.dot_general(q[r], k, NT_DIMS, preferred_element_type=F32) for r in rows]
        if diagonal:
            col = lax.broadcasted_iota(jnp.int32, (sub, T), 1)
            row = lax.broadcasted_iota(jnp.int32, (sub, T), 0)
            s = [jnp.where(col <= row + r.start, x, -jnp.inf) for r, x in zip(rows, s)]
        m_prev = [m_scr[r] for r in rows]
        m_new = [jnp.maximum(mp, jnp.max(x, axis=-1, keepdims=True)) for mp, x in zip(m_prev, s)]
        p = [jnp.exp2(x - mn) for x, mn in zip(s, m_new)]
        alpha = [jnp.exp2(mp - mn) for mp, mn in zip(m_prev, m_new)]
        pv = [jnp.dot(x.astype(BF16), v, preferred_element_type=F32) for x in p]
        for r, x, al, mn, y in zip(rows, p, alpha, m_new, pv):
            l_scr[r] = al * l_scr[r] + jnp.sum(x, axis=-1, keepdims=True)
            acc_scr[r] = al * acc_scr[r] + y
            m_scr[r] = mn

    def body(ki, carry):
        step(ki, False)
        return carry

    lax.fori_loop(0, qi, body, 0)
    step(qi, True)
    o_ref[...] = (acc_scr[...] / l_scr[...]).astype(o_ref.dtype)


def _rope_tables(lay):
    pos = (jnp.arange(lay.tp) - lay.front).astype(F32)
    inv_freq = ROPE_THETA ** (-jnp.arange(0, MLA_D_ROPE, 2, dtype=F32) / MLA_D_ROPE)
    ang = pos[:, None] * inv_freq[None, :]
    cos, sin = jnp.cos(ang), jnp.sin(ang)
    zeros = jnp.zeros((lay.tp, LANES - MLA_D_ROPE), F32)
    return (jnp.concatenate([cos, cos, zeros], axis=1),
            jnp.concatenate([-sin, sin, zeros], axis=1))


def _mla(p_cq, p_ckv, p_kpe, q_norm, wq, kv_norm, wkv, tables, lay):
    qraw = _matmul(p_cq, wq, norm_w=q_norm, name="mla_q_up")
    kvraw = _matmul(p_ckv, wkv, norm_w=kv_norm, name="mla_kv_up")
    rows = lay.rows
    tm = ATTN_TILE
    nt = lay.tp // tm
    hq = MLA_HEADS * MLA_DQK
    hv = MLA_HEADS * MLA_DV
    cos, sin = tables
    q, k, v = pl.pallas_call(
        functools.partial(_mla_prep_kernel, tm=tm, lay=lay),
        out_shape=(jax.ShapeDtypeStruct((rows, hq), BF16),
                   jax.ShapeDtypeStruct((rows, hq), BF16),
                   jax.ShapeDtypeStruct((rows, hv), BF16)),
        grid=(rows // tm,),
        in_specs=[pl.BlockSpec((tm, hq), lambda i: (i, 0)),
                  pl.BlockSpec((tm, MLA_HEADS * (MLA_D_NOPE + MLA_DV)), lambda i: (i, 0)),
                  pl.BlockSpec((tm, LANES), lambda i: (i, 0)),
                  pl.BlockSpec((tm, LANES), lambda i: (i % nt, 0)),
                  pl.BlockSpec((tm, LANES), lambda i: (i % nt, 0))],
        out_specs=(pl.BlockSpec((tm, hq), lambda i: (i, 0)),
                   pl.BlockSpec((tm, hq), lambda i: (i, 0)),
                   pl.BlockSpec((tm, hv), lambda i: (i, 0))),
        compiler_params=_params("arbitrary"),
        name="mla_prep",
    )(qraw, kvraw, p_kpe, cos, sin)
    ta = _tile(lay.tp, FLASH_TILE, ATTN_TILE)
    na = lay.tp // ta
    return pl.pallas_call(
        functools.partial(_flash_kernel, tile=ta),
        out_shape=jax.ShapeDtypeStruct((rows, hv), BF16),
        grid=(lay.batch, MLA_HEADS, na),
        in_specs=[pl.BlockSpec((ta, MLA_DQK), lambda b, h, i: (b * na + i, h)),
                  pl.BlockSpec((lay.tp, MLA_DQK), lambda b, h, i: (b, h)),
                  pl.BlockSpec((lay.tp, MLA_DV), lambda b, h, i: (b, h))],
        out_specs=pl.BlockSpec((ta, MLA_DV), lambda b, h, i: (b * na + i, h)),
        scratch_shapes=[pltpu.VMEM((ta, 1), F32), pltpu.VMEM((ta, 1), F32),
                        pltpu.VMEM((ta, MLA_DV), F32)],
        compiler_params=_params("arbitrary", "arbitrary", "arbitrary"),
        name="mla_attention",
    )(q, k, v)


def _rwkv_kernel(x_ref, mu_ref, wl_ref, w0_ref, a0_ref, kk_ref, ka_ref, rk_ref, lnw_ref, lnb_ref,
                 o_ref, xbuf, s_scr, *, lay):
    c = pl.program_id(1)
    C = CHUNK
    W = RWKV_W
    N = RWKV_N

    @pl.when(c == 0)
    def _():
        xbuf[0:SUBLANES, :] = jnp.zeros((SUBLANES, RWKV_IN_PAD), F32)
        s_scr[...] = jnp.zeros_like(s_scr)

    x = x_ref[...]
    xbuf[SUBLANES:SUBLANES + C, :] = x
    prev = xbuf[SUBLANES - 1:SUBLANES - 1 + C, :]
    xbuf[0:SUBLANES, :] = x[C - SUBLANES:C, :]
    xm = x + (prev - x) * mu_ref[...]
    xm = jnp.where(_valid_rows(c * C, C, lay), xm, 0.0)
    r = xm[:, 0:W]
    k = xm[:, W:2 * W]
    v = xm[:, 2 * W:3 * W]
    lin = xm[:, 3 * W:]
    lane = lax.broadcasted_iota(jnp.int32, lin.shape, 1)
    d1 = RWKV_DECAY_LORA
    d2 = d1 + RWKV_A_LORA
    lin = jnp.where(lane < d1, jnp.tanh(lin),
                    jnp.where(lane < d2, lin, jnp.where(lane < RWKV_LORA, _sigmoid(lin), 0.0)))
    lo = _dot(lin, wl_ref[...])
    w_log = -_softplus(-(w0_ref[...] + lo[:, 0:W])) - 0.5
    lw = -jnp.exp(w_log)
    a = _sigmoid(a0_ref[...] + lo[:, W:2 * W])
    g = lo[:, 2 * W:3 * W]
    kk_raw = k * kk_ref[...]
    k2 = k * (1.0 + (a - 1.0) * ka_ref[...])
    rkr = r * k2 * rk_ref[...]

    row = lax.broadcasted_iota(jnp.int32, (C, C), 0)
    col = lax.broadcasted_iota(jnp.int32, (C, C), 1)
    causal = row >= col
    strict = row > col
    cum = _dot3(jnp.where(causal, 1.0, 0.0), lw)
    c_last = cum[C - 1:C, :]
    e_prev = jnp.exp(cum - lw)
    e_neg = jnp.exp(-cum)
    e_pos = jnp.exp(cum)
    e_rest = jnp.exp(c_last - cum)
    e_last = jnp.exp(c_last)
    lnw = lnw_ref[...]
    lnb = lnb_ref[...]

    heads = range(RWKV_HEADS)
    sl = [slice(h * N, (h + 1) * N) for h in heads]
    kk = [kk_raw[:, x] for x in sl]
    kk = [t * lax.rsqrt(jnp.sum(t * t, axis=-1, keepdims=True) + L2_EPS) for t in kk]
    vh = [v[:, x] for x in sl]
    b_vec = [kk[h] * a[:, sl[h]] for h in heads]
    ar = [jnp.concatenate([-kk[h] * e_prev[:, sl[h]], r[:, sl[h]] * e_pos[:, sl[h]]], axis=0)
          for h in heads]
    bk = [jnp.concatenate([b_vec[h] * e_neg[:, sl[h]], k2[:, sl[h]] * e_neg[:, sl[h]]], axis=0)
          for h in heads]
    big = [_dot(ar[h], bk[h], NT_DIMS) for h in heads]
    a_ab = [jnp.where(strict, t[0:C, 0:C], 0.0) for t in big]
    ak_rk = [jnp.concatenate([jnp.where(strict, t[0:C, C:2 * C], 0.0),
                              jnp.where(causal, t[C:2 * C, C:2 * C], 0.0)], axis=0) for t in big]
    r_b = [jnp.where(causal, t[C:2 * C, 0:C], 0.0) for t in big]
    t_inv = _unit_lower_inverse([-t for t in a_ab], row, col)
    s = [s_scr[h] for h in heads]
    ar_s = [_dot(ar[h], s[h], NT_DIMS) for h in heads]
    akv = [_dot(ak_rk[h], vh[h]) for h in heads]
    u = [_dot(t_inv[h], ar_s[h][0:C] + akv[h][0:C]) for h in heads]
    yh = [ar_s[h][C:2 * C] + akv[h][C:2 * C] + _dot(r_b[h], u[h]) for h in heads]
    upd = [_dot(jnp.concatenate([u[h], vh[h]], axis=0),
                jnp.concatenate([b_vec[h] * e_rest[:, sl[h]], k2[:, sl[h]] * e_rest[:, sl[h]]], axis=0),
                TN_DIMS) for h in heads]
    for h in heads:
        s_scr[h] = s[h] * e_last[:, sl[h]] + upd[h]
        mean = jnp.mean(yh[h], axis=-1, keepdims=True)
        cen = yh[h] - mean
        var = jnp.mean(cen * cen, axis=-1, keepdims=True)
        yn = cen * lax.rsqrt(var + RWKV_LN_EPS) * lnw[:, sl[h]] + lnb[:, sl[h]]
        bonus = jnp.sum(rkr[:, sl[h]], axis=-1, keepdims=True) * vh[h]
        o_ref[:, sl[h]] = ((yn + bonus) * g[:, sl[h]]).astype(o_ref.dtype)


def _rwkv(p_rw, mu, w_lora, w0, a0, k_k, k_a, r_k, ln_w, ln_b, lay):
    C = CHUNK
    nc = lay.tp // C
    vec = lambda t: t.reshape(1, RWKV_W).astype(F32)
    rows = lambda b, c: (b * nc + c, 0)
    const = lambda b, c: (0, 0)
    return pl.pallas_call(
        functools.partial(_rwkv_kernel, lay=lay),
        out_shape=jax.ShapeDtypeStruct((lay.rows, RWKV_W), BF16),
        grid=(lay.batch, nc),
        in_specs=[pl.BlockSpec((C, RWKV_IN_PAD), rows),
                  pl.BlockSpec((1, RWKV_IN_PAD), const),
                  pl.BlockSpec((RWKV_LORA_PAD, 3 * RWKV_W), const)]
                 + [pl.BlockSpec((1, RWKV_W), const)] * 7,
        out_specs=pl.BlockSpec((C, RWKV_W), rows),
        scratch_shapes=[pltpu.VMEM((C + SUBLANES, RWKV_IN_PAD), F32),
                        pltpu.VMEM((RWKV_HEADS, RWKV_N, RWKV_N), F32)],
        compiler_params=_params("arbitrary", "arbitrary"),
        name="rwkv7",
    )(p_rw, jnp.pad(mu.astype(F32), (0, RWKV_IN_PAD - RWKV_IN)).reshape(1, RWKV_IN_PAD), w_lora,
      vec(w0), vec(a0), vec(k_k), vec(k_a), vec(r_k), vec(ln_w), vec(ln_b))


def _rwkv_lora_weight(w2, a2, g2):
    w = jnp.zeros((RWKV_LORA_PAD, 3 * RWKV_W), F32)
    d1 = RWKV_DECAY_LORA
    d2 = d1 + RWKV_A_LORA
    w = w.at[0:d1, 0:RWKV_W].set(w2)
    w = w.at[d1:d2, RWKV_W:2 * RWKV_W].set(a2)
    w = w.at[d2:RWKV_LORA, 2 * RWKV_W:].set(g2)
    return w.astype(BF16)


def _merge_kernel(ya_ref, yb_ref, yc_ref, wb_ref, g0_ref, g1_ref, g2_ref, o_ref):
    acc = _sigmoid(g0_ref[...]) * jnp.dot(ya_ref[...], wb_ref[0], preferred_element_type=F32)
    acc += _sigmoid(g1_ref[...]) * jnp.dot(yb_ref[...], wb_ref[1], preferred_element_type=F32)
    acc += _sigmoid(g2_ref[...]) * jnp.dot(yc_ref[...], wb_ref[2], preferred_element_type=F32)
    o_ref[...] = acc.astype(o_ref.dtype)


def _merge(ya, yb, yc, w_branch, gates):
    rows = ya.shape[0]
    tm = _tile(rows, 512, SUBLANES)
    tn = 512
    nj = D_MODEL // tn
    y_spec = pl.BlockSpec((tm, BRANCH_W), lambda i, j: (i, 0))
    gate = lambda br: pl.BlockSpec((tm, tn), lambda i, j: (i, br * nj + j))
    return pl.pallas_call(
        _merge_kernel,
        out_shape=jax.ShapeDtypeStruct((rows, D_MODEL), BF16),
        grid=(rows // tm, nj),
        in_specs=[y_spec, y_spec, y_spec,
                  pl.BlockSpec((N_BRANCH, BRANCH_W, tn), lambda i, j: (0, 0, j)),
                  gate(0), gate(1), gate(2)],
        out_specs=pl.BlockSpec((tm, tn), lambda i, j: (i, j)),
        compiler_params=_params("arbitrary", "arbitrary"),
        name="gated_merge",
    )(ya, yb, yc, w_branch, gates, gates, gates)


def _router_kernel(h_ref, nw_ref, wr_ref, br_ref, xt_ref, idx_ref, gw_ref, *, tm, lay):
    i = pl.program_id(0)
    x = h_ref[...]
    x = x * lax.rsqrt(jnp.mean(x * x, axis=-1, keepdims=True) + NORM_EPS) * nw_ref[...]
    x = jnp.where(_valid_rows(lax.rem(i * tm, lay.tp), tm, lay), x, 0.0)
    xt_ref[...] = x
    logits = _dot3(x, wr_ref[...]) + br_ref[...]
    lane = lax.broadcasted_iota(jnp.int32, logits.shape, 1)
    neg = -jnp.inf

    def top1(vals):
        m = jnp.max(vals, axis=-1, keepdims=True)
        idx = jnp.min(jnp.where(vals == m, lane, LANES), axis=-1, keepdims=True)
        return m, idx

    g_mask = lane < N_GROUPS
    g_top, g_idx = top1(jnp.where(g_mask, logits, neg))
    p_group = 1.0 / jnp.sum(jnp.where(g_mask, jnp.exp(logits - g_top), 0.0), axis=-1, keepdims=True)
    e_lo = N_GROUPS + g_idx * EXPERTS_PER_GROUP
    e_vals = jnp.where((lane >= e_lo) & (lane < e_lo + EXPERTS_PER_GROUP), logits, neg)
    e1, i1 = top1(e_vals)
    e2, i2 = top1(jnp.where(lane == i1, neg, e_vals))
    t = jnp.exp(e2 - e1)
    w1 = p_group / (1.0 + t)
    w2 = p_group * t / (1.0 + t)
    idx_ref[...] = jnp.where(lane == 0, i1 - N_GROUPS, jnp.where(lane == 1, i2 - N_GROUPS, 0))
    gw_ref[...] = jnp.where(lane == 0, w1, jnp.where(lane == 1, w2, 0.0))


def _router(h, norm_w, w_r, b_r, lay):
    rows = h.shape[0]
    tm = _tile(rows, 512, SUBLANES)
    row_spec = lambda w: pl.BlockSpec((tm, w), lambda i: (i, 0))
    return pl.pallas_call(
        functools.partial(_router_kernel, tm=tm, lay=lay),
        out_shape=(jax.ShapeDtypeStruct((rows, D_MODEL), F32),
                   jax.ShapeDtypeStruct((rows, LANES), jnp.int32),
                   jax.ShapeDtypeStruct((rows, LANES), F32)),
        grid=(rows // tm,),
        in_specs=[row_spec(D_MODEL),
                  pl.BlockSpec((1, D_MODEL), lambda i: (0, 0)),
                  pl.BlockSpec((D_MODEL, LANES), lambda i: (0, 0)),
                  pl.BlockSpec((1, LANES), lambda i: (0, 0))],
        out_specs=(row_spec(D_MODEL), row_spec(LANES), row_spec(LANES)),
        compiler_params=_params("arbitrary"),
        name="moe_router",
    )(h, norm_w.reshape(1, D_MODEL).astype(F32), w_r, b_r)


def _row_gather(idx_ref, base, n, src_hbm, dst, sem):
    def body(r, carry):
        tok = idx_ref[base + r]
        pltpu.make_async_copy(src_hbm.at[pl.ds(tok, 1), :], dst.at[pl.ds(r, 1), :], sem).start()
        return carry
    lax.fori_loop(0, n, body, 0, unroll=GATHER_UNROLL)


def _row_gather_wait(n, src_hbm, dst, sem):
    def body(r, carry):
        pltpu.make_async_copy(src_hbm.at[pl.ds(0, 1), :], dst.at[pl.ds(r, 1), :], sem).wait()
        return carry
    lax.fori_loop(0, n, body, 0, unroll=GATHER_UNROLL)


def _expert_kernel(be_ref, tok_ref, nused_ref, xt_hbm, wg_ref, wu_ref, wd_ref, o_ref,
                   xbuf, wg_bf, wu_bf, wd_bf, sem):
    i = pl.program_id(0)
    nb = pl.num_programs(0)
    tb = MOE_TILE
    slot = i % 2
    used = nused_ref[0]

    @pl.when((i == 0) & (used > 0))
    def _():
        _row_gather(tok_ref, 0, tb, xt_hbm, xbuf.at[0], sem.at[0])

    @pl.when((i + 1 < nb) & (i + 1 < used))
    def _():
        _row_gather(tok_ref, (i + 1) * tb, tb, xt_hbm, xbuf.at[1 - slot], sem.at[1 - slot])

    @pl.when(i < used)
    def _():
        @pl.when((i == 0) | (be_ref[i] != be_ref[jnp.maximum(i - 1, 0)]))
        def _():
            wg_bf[...] = wg_ref[0].astype(BF16)
            wu_bf[...] = wu_ref[0].astype(BF16)
            wd_bf[...] = wd_ref[0].astype(BF16)

        _row_gather_wait(tb, xt_hbm, xbuf.at[slot], sem.at[slot])
        x = xbuf[slot].astype(BF16)
        gate = jnp.dot(x, wg_bf[...], preferred_element_type=F32)
        up = jnp.dot(x, wu_bf[...], preferred_element_type=F32)
        act = (gate * _sigmoid(gate) * up).astype(BF16)
        o_ref[...] = jnp.dot(act, wd_bf[...], preferred_element_type=F32)

    @pl.when(i >= used)
    def _():
        o_ref[...] = jnp.zeros_like(o_ref)


def _experts(xt, block_expert, row_token, n_used, w_gate, w_up, w_down, layer):
    n_rows = row_token.shape[0]
    nb = n_rows // MOE_TILE
    first = layer * N_EXPERTS
    w_in_spec = pl.BlockSpec((1, D_MODEL, D_EXPERT), lambda i, be, tok, nu: (first + be[i], 0, 0))
    grid_spec = pltpu.PrefetchScalarGridSpec(
        num_scalar_prefetch=3,
        grid=(nb,),
        in_specs=[pl.BlockSpec(memory_space=pl.ANY), w_in_spec, w_in_spec,
                  pl.BlockSpec((1, D_EXPERT, D_MODEL), lambda i, be, tok, nu: (first + be[i], 0, 0))],
        out_specs=pl.BlockSpec((MOE_TILE, D_MODEL), lambda i, be, tok, nu: (i, 0)),
        scratch_shapes=[pltpu.VMEM((2, MOE_TILE, D_MODEL), F32),
                        pltpu.VMEM((D_MODEL, D_EXPERT), BF16),
                        pltpu.VMEM((D_MODEL, D_EXPERT), BF16),
                        pltpu.VMEM((D_EXPERT, D_MODEL), BF16),
                        pltpu.SemaphoreType.DMA((2,))],
    )
    return pl.pallas_call(
        _expert_kernel,
        out_shape=jax.ShapeDtypeStruct((n_rows, D_MODEL), F32),
        grid_spec=grid_spec,
        compiler_params=_params("arbitrary"),
        name="moe_experts",
    )(block_expert, row_token, n_used, xt, w_gate, w_up, w_down)


def _combine_kernel(dest0_ref, dest1_ref, y_hbm, h_ref, gw_ref, o_ref, buf0, buf1, sem):
    i = pl.program_id(0)
    nb = pl.num_programs(0)
    tc = COMBINE_TILE
    slot = i % 2

    def start(step, s):
        _row_gather(dest0_ref, step * tc, tc, y_hbm, buf0.at[s], sem.at[0, s])
        _row_gather(dest1_ref, step * tc, tc, y_hbm, buf1.at[s], sem.at[1, s])

    @pl.when(i == 0)
    def _():
        start(0, 0)

    @pl.when(i + 1 < nb)
    def _():
        start(i + 1, 1 - slot)

    _row_gather_wait(tc, y_hbm, buf0.at[slot], sem.at[0, slot])
    _row_gather_wait(tc, y_hbm, buf1.at[slot], sem.at[1, slot])
    gw = gw_ref[...]
    o_ref[...] = h_ref[...] + gw[:, 0:1] * buf0[slot] + gw[:, 1:2] * buf1[slot]


def _combine(h, y_rows, dest0, dest1, gw):
    rows = h.shape[0]
    tc = COMBINE_TILE
    grid_spec = pltpu.PrefetchScalarGridSpec(
        num_scalar_prefetch=2,
        grid=(rows // tc,),
        in_specs=[pl.BlockSpec(memory_space=pl.ANY),
                  pl.BlockSpec((tc, D_MODEL), lambda i, d0, d1: (i, 0)),
                  pl.BlockSpec((tc, LANES), lambda i, d0, d1: (i, 0))],
        out_specs=pl.BlockSpec((tc, D_MODEL), lambda i, d0, d1: (i, 0)),
        scratch_shapes=[pltpu.VMEM((2, tc, D_MODEL), F32),
                        pltpu.VMEM((2, tc, D_MODEL), F32),
                        pltpu.SemaphoreType.DMA((2, 2))],
    )
    return pl.pallas_call(
        _combine_kernel,
        out_shape=jax.ShapeDtypeStruct((rows, D_MODEL), F32),
        grid_spec=grid_spec,
        compiler_params=_params("arbitrary"),
        name="moe_combine",
    )(dest0, dest1, y_rows, h, gw)


def _expert_ranks(expert):
    n = expert.shape[0]
    blk = 2 * MOE_TILE
    onehot = (expert[:, None] == jnp.arange(N_EXPERTS, dtype=jnp.int32)[None, :]).astype(F32)
    oh = onehot.reshape(n // blk, blk, N_EXPERTS)
    within = jnp.einsum("ts,bse->bte", jnp.tril(jnp.ones((blk, blk), F32)), oh)
    totals = within[:, -1, :]
    before = jnp.cumsum(totals, axis=0) - totals
    csum = (within + before[:, None, :]).reshape(n, N_EXPERTS)
    rank = jnp.sum(csum * onehot, axis=1).astype(jnp.int32) - 1
    return rank, csum[-1].astype(jnp.int32)


def _moe(h, norm_w, w_r, b_r, w_gate, w_up, w_down, layer, lay):
    rows = h.shape[0]
    xt, idx, gw = _router(h, norm_w, w_r, b_r, lay)
    expert = idx[:, :TOP_K].reshape(-1)
    n_assign = rows * TOP_K
    rank, counts = _expert_ranks(expert)
    padded = (counts + MOE_TILE - 1) // MOE_TILE * MOE_TILE
    padded_end = jnp.cumsum(padded)
    padded_start = padded_end - padded
    dest = (padded_start[expert] + rank).astype(jnp.int32)
    n_rows = (-(-n_assign // MOE_TILE) + N_EXPERTS) * MOE_TILE
    nb = n_rows // MOE_TILE
    row_token = jnp.zeros((n_rows,), jnp.int32).at[dest].set(
        jnp.arange(n_assign, dtype=jnp.int32) // TOP_K, unique_indices=True)
    block_start = jnp.arange(nb, dtype=jnp.int32) * MOE_TILE
    block_expert = jnp.minimum(
        jnp.sum((padded_end[None, :] <= block_start[:, None]).astype(jnp.int32), axis=1),
        N_EXPERTS - 1).astype(jnp.int32)
    n_used = (padded_end[-1:] // MOE_TILE).astype(jnp.int32)
    y_rows = _experts(xt, block_expert, row_token, n_used, w_gate, w_up, w_down, layer)
    dest2 = dest.reshape(rows, TOP_K)
    return _combine(h, y_rows, dest2[:, 0], dest2[:, 1], gw)


def _final_norm_kernel(h_ref, w_ref, o_ref):
    x = h_ref[...]
    o_ref[...] = x * lax.rsqrt(jnp.mean(x * x, axis=-1, keepdims=True) + NORM_EPS) * w_ref[...]


def _final_norm(h, w, lay, seq):
    first = lay.front + N_META
    tm = math.gcd(math.gcd(first, seq), lay.tp)
    per_b = lay.tp // tm
    nseq = seq // tm
    return pl.pallas_call(
        _final_norm_kernel,
        out_shape=jax.ShapeDtypeStruct((lay.batch * seq, D_MODEL), F32),
        grid=(lay.batch, nseq),
        in_specs=[pl.BlockSpec((tm, D_MODEL), lambda b, i: (b * per_b + first // tm + i, 0)),
                  pl.BlockSpec((1, D_MODEL), lambda b, i: (0, 0))],
        out_specs=pl.BlockSpec((tm, D_MODEL), lambda b, i: (b * nseq + i, 0)),
        compiler_params=_params("arbitrary", "arbitrary"),
        name="final_norm",
    )(h, w.reshape(1, D_MODEL).astype(F32))


def _pad_cols(w, width):
    return jnp.pad(w, ((0, 0), (0, width - w.shape[1])))


def _layer_weights(w_in, mla_w_q_up, mla_w_kv_up, w2, a2, g2, w_branch, w_out,
                   w_rg, b_rg, w_re, b_re):
    offs = [0]
    for s in IN_SPLITS:
        offs.append(offs[-1] + s)
    seg = lambda i, j: w_in[:, offs[i]:offs[j]]
    w_gdn = jnp.concatenate([seg(0, 4), _pad_cols(seg(4, 6), LANES)], axis=1).astype(BF16)
    w_cq = seg(6, 7).astype(BF16)
    w_ckv = seg(7, 8).astype(BF16)
    w_kpe = _pad_cols(seg(8, 9), LANES).astype(BF16)
    w_rw = _pad_cols(seg(9, 10), RWKV_IN_PAD).astype(BF16)
    w_gates = seg(10, 11).astype(BF16)
    wq = mla_w_q_up.reshape(MLA_Q_LORA, MLA_HEADS, MLA_D_NOPE + MLA_D_ROPE)
    wq = jnp.pad(wq, ((0, 0), (0, 0), (0, MLA_DQK - MLA_D_NOPE - MLA_D_ROPE)))
    wq = wq.reshape(MLA_Q_LORA, MLA_HEADS * MLA_DQK).astype(BF16)
    wkv = mla_w_kv_up.reshape(MLA_KV_LORA, MLA_HEADS, MLA_D_NOPE + MLA_DV)
    wkv = jnp.concatenate([wkv[:, :, :MLA_D_NOPE].reshape(MLA_KV_LORA, -1),
                           wkv[:, :, MLA_D_NOPE:].reshape(MLA_KV_LORA, -1)], axis=1).astype(BF16)
    w_r = _pad_cols(jnp.concatenate([w_rg, w_re], axis=1), LANES).astype(F32)
    b_r = jnp.pad(jnp.concatenate([b_rg, b_re]), (0, LANES - N_GROUPS - N_EXPERTS)).reshape(1, LANES)
    return dict(
        w_gdn=w_gdn, w_cq=w_cq, w_ckv=w_ckv, w_kpe=w_kpe, w_rw=w_rw, w_gates=w_gates,
        wq=wq, wkv=wkv, w_lora=_rwkv_lora_weight(w2, a2, g2),
        w_branch=w_branch.astype(BF16), w_out=w_out.astype(BF16), w_r=w_r, b_r=b_r.astype(F32))


def _mixer(h, norm_w, lw, gdn_p, mla_p, rwkv_p, tables, lay):
    proj = lambda w, name: _matmul(h, w, norm_w=norm_w, lay=lay, name=name)
    ya = _gdn(proj(lw["w_gdn"], "proj_gdn"), *gdn_p, lay)
    yb = _mla(proj(lw["w_cq"], "proj_cq"), proj(lw["w_ckv"], "proj_ckv"), proj(lw["w_kpe"], "proj_kpe"),
              mla_p[0], lw["wq"], mla_p[1], lw["wkv"], tables, lay)
    yc = _rwkv(proj(lw["w_rw"], "proj_rwkv"), rwkv_p[0], lw["w_lora"], *rwkv_p[1:], lay)
    merged = _merge(ya, yb, yc, lw["w_branch"], proj(lw["w_gates"], "proj_gates"))
    return _matmul(merged, lw["w_out"], residual=h, name="mixer_out")


def kernel(x, meta_tokens, norm_mix, w_in, gdn_conv, gdn_a_log, gdn_dt_bias, gdn_norm, mla_q_norm, mla_w_q_up, mla_kv_norm, mla_w_kv_up, rwkv_mu, rwkv_w0, rwkv_w2, rwkv_a0, rwkv_a2, rwkv_g2, rwkv_k_k, rwkv_k_a, rwkv_r_k, rwkv_ln_w, rwkv_ln_b, w_branch, w_out, norm_ffn, w_router_group, b_router_group, w_router_expert, b_router_expert, w_exp_gate, w_exp_up, w_exp_down, norm_final):
    batch, seq, _ = x.shape
    depth = w_in.shape[0]
    lay = make_layout(batch, seq)
    meta = jnp.broadcast_to(meta_tokens[None].astype(x.dtype), (batch, N_META, D_MODEL))
    h = jnp.concatenate([jnp.zeros((batch, lay.front, D_MODEL), x.dtype), meta, x,
                         jnp.zeros((batch, lay.tp - lay.front - lay.t, D_MODEL), x.dtype)], axis=1)
    h = h.reshape(lay.rows, D_MODEL)
    tables = _rope_tables(lay)
    w_eg = w_exp_gate.reshape(depth * N_EXPERTS, D_MODEL, D_EXPERT)
    w_eu = w_exp_up.reshape(depth * N_EXPERTS, D_MODEL, D_EXPERT)
    w_ed = w_exp_down.reshape(depth * N_EXPERTS, D_EXPERT, D_MODEL)
    for l in range(depth):
        lw = _layer_weights(w_in[l], mla_w_q_up[l], mla_w_kv_up[l], rwkv_w2[l], rwkv_a2[l], rwkv_g2[l],
                            w_branch[l], w_out[l], w_router_group[l], b_router_group[l],
                            w_router_expert[l], b_router_expert[l])
        h = _mixer(h, norm_mix[l], lw,
                   (gdn_conv[l], gdn_a_log[l], gdn_dt_bias[l], gdn_norm[l]),
                   (mla_q_norm[l], mla_kv_norm[l]),
                   (rwkv_mu[l], rwkv_w0[l], rwkv_a0[l], rwkv_k_k[l], rwkv_k_a[l],
                    rwkv_r_k[l].reshape(-1), rwkv_ln_w[l], rwkv_ln_b[l]),
                   tables, lay)
        h = _moe(h, norm_ffn[l], lw["w_r"], lw["b_r"], w_eg, w_eu, w_ed, l, lay)
    return _final_norm(h, norm_final, lay, seq).reshape(batch, seq, D_MODEL)
```
